```python
import math
import jax, jax.numpy as jnp
from jax import lax
import numpy as np


D_MODEL = 1024
BATCH = 2
SEQ = 16384
DEPTH = 2

MEM_LEN = 256
SB_HEADS = 4
SB_HEAD_DIM = 128
SB_WIDTH = SB_HEADS * SB_HEAD_DIM
Q_BLOCK = 128
POOL_WINDOWS = (2, 4, 8, 16)
POOL_GROUPS = 4
POOL_GROUP_DIM = 128
POOL_WIDTH = POOL_GROUPS * POOL_GROUP_DIM
LRU_BLOCKS = 8
LRU_BLOCK_DIM = 64
LRU_WIDTH = LRU_BLOCKS * LRU_BLOCK_DIM
CONV_WIDTH = 4
LRU_C = 8.0
XA_HEADS = 4
XA_HEAD_DIM = 128
XA_WIDTH = XA_HEADS * XA_HEAD_DIM
N_BRANCH = 4
BRANCH_WIDTH = 512
DN_ALPHA = (2 * DEPTH) ** 0.25
DN_BETA = (8 * DEPTH) ** -0.25
LN_EPS = 1e-5
SPLIT_SIZES = (SB_WIDTH, SB_WIDTH, SB_WIDTH, SB_WIDTH, POOL_WIDTH, POOL_WIDTH, LRU_WIDTH, LRU_WIDTH, XA_WIDTH, XA_WIDTH, N_BRANCH * D_MODEL)
IN_WIDTH = 10 * 512 + N_BRANCH * D_MODEL

kernel_name = 'hybrid_stickbreak_pool_rglru_memxattn'


def _split_columns(p):
    out, off = [], 0
    for size in SPLIT_SIZES:
        out.append(p[..., off:off + size])
        off += size
    return out


def layer_norm(x, g, b):
    xf = x.astype(jnp.float32)
    mean = jnp.mean(xf, axis=-1, keepdims=True)
    var = jnp.mean(jnp.square(xf - mean), axis=-1, keepdims=True)
    y = (xf - mean) * lax.rsqrt(var + LN_EPS) * g.astype(jnp.float32) + b.astype(jnp.float32)
    return y.astype(x.dtype)


def stick_breaking_attention(q, k, v):
    b, s, h, d = q.shape
    nb = s // Q_BLOCK
    scale = 1.0 / math.sqrt(d)
    qh = q.transpose(0, 2, 1, 3)
    kh = k.transpose(0, 2, 1, 3).reshape(b, h, nb, Q_BLOCK, d)
    vh = v.transpose(0, 2, 1, 3).reshape(b, h, nb, Q_BLOCK, d)
    loc = jnp.arange(Q_BLOCK)
    suffix = (loc[:, None] > loc[None, :]).astype(jnp.float32)
    diag_valid = loc[None, :] < loc[:, None]
    outs = []
    for i in range(nb):
        nk = i + 1
        q_blk = qh[:, :, i * Q_BLOCK:(i + 1) * Q_BLOCK]
        k_pre = kh[:, :, :nk]
        v_pre = vh[:, :, :nk]
        z = jnp.einsum('bhqd,bhnkd->bhqnk', q_blk, k_pre, preferred_element_type=jnp.float32) * scale
        blk_idx = jnp.arange(nk)
        valid = (blk_idx[None, :, None] < i) | diag_valid[:, None, :]
        log_1m = jnp.where(valid, jax.nn.log_sigmoid(-z), 0.0)
        within = jnp.einsum('bhqnk,kj->bhqnj', log_1m, suffix)
        totals = jnp.sum(log_1m, axis=-1)
        later = lax.cumsum(totals, axis=3, reverse=True) - totals
        w = jnp.where(valid, jnp.exp(jax.nn.log_sigmoid(z) + within + later[..., None]), 0.0)
        o = jnp.einsum('bhqnk,bhnkd->bhqd', w.astype(v.dtype), v_pre, preferred_element_type=jnp.float32)
        outs.append(o.astype(v.dtype))
    out = jnp.concatenate(outs, axis=2)
    return out.transpose(0, 2, 1, 3).reshape(b, s, h * d)


def multiscale_pool(u, w_pool, pool_scale):
    b, s, c = u.shape
    uf = u.astype(jnp.float32)
    csum = jnp.concatenate([jnp.zeros((b, 1, c), jnp.float32), jnp.cumsum(uf, axis=1)], axis=1)
    pos = jnp.arange(s)
    groups = []
    for g, win in enumerate(POOL_WINDOWS):
        sl = slice(g * POOL_GROUP_DIM, (g + 1) * POOL_GROUP_DIM)
        start = jnp.maximum(pos + 1 - win, 0)
        window_sum = csum[:, 1:, sl] - csum[:, start, sl]
        count = (pos + 1 - start).astype(jnp.float32)[None, :, None]
        groups.append(window_sum / count - uf[:, :, sl])
    pooled = jnp.stack(groups, axis=2)
    mixed = jnp.einsum('bsgi,gij->bsgj', pooled, w_pool.astype(jnp.float32)).reshape(b, s, c)
    return (mixed * pool_scale.astype(jnp.float32)).astype(u.dtype)


def rg_lru(u, conv_w, conv_b, w_rg, b_rg, w_ig, b_ig, lru_L):
    b, s, c = u.shape
    xc = lax.conv_general_dilated(u, conv_w[:, None, :], window_strides=(1,), padding=[(CONV_WIDTH - 1, 0)],
                                  dimension_numbers=('NWC', 'WIO', 'NWC'), feature_group_count=c) + conv_b
    xb = xc.reshape(b, s, LRU_BLOCKS, LRU_BLOCK_DIM)
    r = jax.nn.sigmoid(jnp.einsum('bshi,hij->bshj', xb, w_rg).reshape(b, s, c) + b_rg)
    i = jax.nn.sigmoid(jnp.einsum('bshi,hij->bshj', xb, w_ig).reshape(b, s, c) + b_ig)
    log_a = (-LRU_C * r.astype(jnp.float32)) * jax.nn.softplus(-lru_L.astype(jnp.float32))
    a = jnp.exp(log_a)
    mult = jnp.sqrt(-jnp.expm1(2.0 * log_a))
    bx = mult * (i * xc).astype(jnp.float32)

    def combine(left, right):
        a_l, b_l = left
        a_r, b_r = right
        return a_l * a_r, a_r * b_l + b_r

    _, h = lax.associative_scan(combine, (a, bx), axis=1)
    return h.astype(u.dtype)


def memory_cross_attention(q, mem, w_mem_kv):
    b, s, _ = q.shape
    m = mem.shape[1]
    kv = mem @ w_mem_kv
    k = kv[..., :XA_WIDTH].reshape(b, m, XA_HEADS, XA_HEAD_DIM)
    v = kv[..., XA_WIDTH:].reshape(b, m, XA_HEADS, XA_HEAD_DIM)
    qh = q.reshape(b, s, XA_HEADS, XA_HEAD_DIM)
    scores = jnp.einsum('bshd,bmhd->bhsm', qh, k, preferred_element_type=jnp.float32) / math.sqrt(XA_HEAD_DIM)
    p = jax.nn.softmax(scores, axis=-1)
    out = jnp.einsum('bhsm,bmhd->bshd', p.astype(v.dtype), v)
    return out.reshape(b, s, XA_WIDTH)


def hybrid_layer(x, mem, w_in, w_pool, pool_scale, conv_w, conv_b, w_rg, b_rg, w_ig, b_ig, lru_L,
                 w_mem_kv, w_branch, w_out, ln_g, ln_b):
    b, s, d = x.shape
    proj = x @ w_in
    (q_sb, k_sb, v_sb, g_sb, u_pool, g_pool, u_lru, g_lru, q_xa, g_xa, merge) = _split_columns(proj)
    hs = (b, s, SB_HEADS, SB_HEAD_DIM)
    y_sb = stick_breaking_attention(q_sb.reshape(hs), k_sb.reshape(hs), v_sb.reshape(hs)) * jax.nn.silu(g_sb)
    y_pool = multiscale_pool(u_pool, w_pool, pool_scale) * jax.nn.silu(g_pool)
    y_lru = rg_lru(u_lru, conv_w, conv_b, w_rg, b_rg, w_ig, b_ig, lru_L) * jax.nn.silu(g_lru)
    y_xa = memory_cross_attention(q_xa, mem, w_mem_kv) * jax.nn.silu(g_xa)
    gates = jax.nn.sigmoid(merge.reshape(b, s, N_BRANCH, d))
    branches = (y_sb, y_pool, y_lru, y_xa)
    merged = gates[:, :, 0] * (branches[0] @ w_branch[0])
    for n in range(1, N_BRANCH):
        merged = merged + gates[:, :, n] * (branches[n] @ w_branch[n])
    out = merged @ w_out
    return layer_norm(DN_ALPHA * x + out, ln_g, ln_b)


def setup_inputs(seed: int = 0) -> dict:
    key = jax.random.key(seed)
    ks = jax.random.split(key, 20)
    f32 = jnp.float32
    nrm = lambda k, shape, fan_in, mult=1.0: jax.random.normal(k, shape, f32) * (fan_in ** -0.5) * mult
    x = jax.random.normal(ks[0], (BATCH, SEQ, D_MODEL), f32)
    mem = jax.random.normal(ks[1], (BATCH, MEM_LEN, D_MODEL), f32)
    w_in = nrm(ks[2], (DEPTH, D_MODEL, IN_WIDTH), D_MODEL)
    w_pool = nrm(ks[3], (DEPTH, POOL_GROUPS, POOL_GROUP_DIM, POOL_GROUP_DIM), POOL_GROUP_DIM)
    pool_scale = 1.0 + 0.02 * jax.random.normal(ks[4], (DEPTH, POOL_WIDTH), f32)
    conv_w = nrm(ks[5], (DEPTH, CONV_WIDTH, LRU_WIDTH), CONV_WIDTH)
    conv_b = 0.01 * jax.random.normal(ks[6], (DEPTH, LRU_WIDTH), f32)
    w_rg = nrm(ks[7], (DEPTH, LRU_BLOCKS, LRU_BLOCK_DIM, LRU_BLOCK_DIM), LRU_BLOCK_DIM)
    b_rg = 0.01 * jax.random.normal(ks[8], (DEPTH, LRU_WIDTH), f32)
    w_ig = nrm(ks[9], (DEPTH, LRU_BLOCKS, LRU_BLOCK_DIM, LRU_BLOCK_DIM), LRU_BLOCK_DIM)
    b_ig = 0.01 * jax.random.normal(ks[10], (DEPTH, LRU_WIDTH), f32)
    a_c = jax.random.uniform(ks[11], (DEPTH, LRU_WIDTH), f32, minval=0.9, maxval=0.999)
    sig = a_c ** (1.0 / LRU_C)
    lru_L = jnp.log(sig) - jnp.log1p(-sig)
    w_mem_kv = nrm(ks[12], (DEPTH, D_MODEL, 2 * XA_WIDTH), D_MODEL)
    w_branch = nrm(ks[13], (DEPTH, N_BRANCH, BRANCH_WIDTH, D_MODEL), BRANCH_WIDTH, DN_BETA)
    w_out = nrm(ks[14], (DEPTH, D_MODEL, D_MODEL), D_MODEL, DN_BETA)
    ln_g = 1.0 + 0.02 * jax.random.normal(ks[15], (DEPTH, D_MODEL), f32)
    ln_b = 0.02 * jax.random.normal(ks[16], (DEPTH, D_MODEL), f32)
    return {'x': x, 'mem': mem, 'w_in': w_in, 'w_pool': w_pool, 'pool_scale': pool_scale,
            'conv_w': conv_w, 'conv_b': conv_b, 'w_rg': w_rg, 'b_rg': b_rg, 'w_ig': w_ig, 'b_ig': b_ig,
            'lru_L': lru_L, 'w_mem_kv': w_mem_kv, 'w_branch': w_branch, 'w_out': w_out,
            'ln_g': ln_g, 'ln_b': ln_b}


def reference(x, mem, w_in, w_pool, pool_scale, conv_w, conv_b, w_rg, b_rg, w_ig, b_ig, lru_L,
              w_mem_kv, w_branch, w_out, ln_g, ln_b):
    for l in range(DEPTH):
        x = hybrid_layer(x, mem, w_in[l], w_pool[l], pool_scale[l], conv_w[l], conv_b[l], w_rg[l], b_rg[l],
                         w_ig[l], b_ig[l], lru_L[l], w_mem_kv[l], w_branch[l], w_out[l], ln_g[l], ln_b[l])
    return x
```

```python
import functools
import math

import jax
import jax.numpy as jnp
from jax import lax
from jax.experimental import pallas as pl
from jax.experimental.pallas import tpu as pltpu

F32 = jnp.float32
BF16 = jnp.bfloat16

HEAD_DIM = 128
N_HEADS = 4
WIDTH = N_HEADS * HEAD_DIM
N_BRANCH = 4
POOL_WINDOWS = (2, 4, 8, 16)
LRU_BLOCK_DIM = 64
LRU_C = 8.0
CONV_WIDTH = 4
LN_EPS = 1e-5
HALO = 16
LOG2E = 1.4426950408889634
VMEM_LIMIT = 56 * 1024 * 1024


def _cparams(sem):
    return pltpu.CompilerParams(dimension_semantics=sem, vmem_limit_bytes=VMEM_LIMIT)


def _mm_kernel(x_ref, w_ref, s_ref, o_ref):
    acc = jnp.dot(x_ref[...].astype(BF16), w_ref[...], preferred_element_type=F32)
    o_ref[...] = (acc * s_ref[...]).astype(o_ref.dtype)


def _matmul(x, w, scale, out_dtype, tm, tn):
    m, k = x.shape
    n = w.shape[1]
    tm, tn = min(tm, m), min(tn, n)
    return pl.pallas_call(
        _mm_kernel,
        grid=(m // tm, n // tn),
        in_specs=[pl.BlockSpec((tm, k), lambda i, j: (i, 0)),
                  pl.BlockSpec((k, tn), lambda i, j: (0, j)),
                  pl.BlockSpec((1, tn), lambda i, j: (0, j))],
        out_specs=pl.BlockSpec((tm, tn), lambda i, j: (i, j)),
        out_shape=jax.ShapeDtypeStruct((m, n), out_dtype),
        compiler_params=_cparams(("parallel", "parallel")),
        name="proj_matmul",
    )(x, w, scale)


def _sb_kernel(q_ref, k_ref, v_ref, t_ref, o_ref, acc_ref, later_ref, *, tq):
    i = pl.program_id(2)
    q = q_ref[...]
    tmat = t_ref[...]
    row = lax.broadcasted_iota(jnp.int32, (tq, tq), 0)
    col = lax.broadcasted_iota(jnp.int32, (tq, tq), 1)
    valid = col < row
    acc_ref[...] = jnp.zeros_like(acc_ref)
    later_ref[...] = jnp.zeros_like(later_ref)

    def tile(j, masked):
        start = pl.multiple_of(j * tq, tq)
        k = k_ref[pl.ds(start, tq), :]
        v = v_ref[pl.ds(start, tq), :]
        z = lax.dot_general(q, k, (((1,), (1,)), ((), ())), preferred_element_type=F32)
        sp = jnp.maximum(z, 0.0) + jnp.log2(1.0 + jnp.exp2(-jnp.abs(z)))
        if masked:
            sp = jnp.where(valid, sp, 0.0)
        hi = sp.astype(BF16)
        lo = (sp - hi.astype(F32)).astype(BF16)
        incl = (jnp.dot(hi, tmat, preferred_element_type=F32)
                + jnp.dot(lo, tmat, preferred_element_type=F32))
        later = later_ref[...]
        expo = z - incl[:, :tq] + jnp.concatenate([later] * (tq // HEAD_DIM), axis=1)
        w = jnp.exp2(expo)
        if masked:
            w = jnp.where(valid, w, 0.0)
        acc_ref[...] += jnp.dot(w.astype(BF16), v, preferred_element_type=F32)
        later_ref[...] = later - incl[:, tq:]

    tile(i, True)

    def body(n, carry):
        tile(i - 1 - n, False)
        return carry

    lax.fori_loop(0, i, body, 0)
    o_ref[...] = acc_ref[...]


def _sb_attention(qkv, tq):
    b, s, _ = qkv.shape
    tq = min(tq, s)
    loc = jnp.arange(tq)
    tri = (loc[:, None] >= loc[None, :]).astype(BF16)
    tmat = jnp.concatenate([tri, jnp.ones((tq, HEAD_DIM), BF16)], axis=1)
    return pl.pallas_call(
        functools.partial(_sb_kernel, tq=tq),
        grid=(b, N_HEADS, s // tq),
        in_specs=[pl.BlockSpec((None, tq, HEAD_DIM), lambda bi, h, i: (bi, i, h)),
                  pl.BlockSpec((None, s, HEAD_DIM), lambda bi, h, i: (bi, 0, N_HEADS + h)),
                  pl.BlockSpec((None, s, HEAD_DIM), lambda bi, h, i: (bi, 0, 2 * N_HEADS + h)),
                  pl.BlockSpec((tq, tq + HEAD_DIM), lambda bi, h, i: (0, 0))],
        out_specs=pl.BlockSpec((None, tq, HEAD_DIM), lambda bi, h, i: (bi, i, h)),
        out_shape=jax.ShapeDtypeStruct((b, s, WIDTH), F32),
        scratch_shapes=[pltpu.VMEM((tq, HEAD_DIM), F32), pltpu.VMEM((tq, HEAD_DIM), F32)],
        compiler_params=_cparams(("parallel", "parallel", "arbitrary")),
        name="stickbreak_attention",
    )(qkv, qkv, qkv, tmat)


def _sigmoid(x):
    return 1.0 / (1.0 + jnp.exp(-x))


def _silu(x):
    return x * _sigmoid(x)


def _mixer_kernel(merge_ref, gate_ref, u_ref, halo_ref, qx_ref, ysb_ref, x_ref, kv_ref,
                  wpool_ref, pscale_ref, convw_ref, convb_ref, wrg_ref, brg_ref, wig_ref, big_ref,
                  lrul_ref, wb_ref, wout_ref, lng_ref, lnb_ref, o_ref, h_ref, *, t, alpha):
    i = pl.program_id(1)

    @pl.when(i == 0)
    def _():
        h_ref[...] = jnp.zeros_like(h_ref)

    u = u_ref[...]
    halo = jnp.where(i == 0, 0.0, halo_ref[...])
    ext = jnp.concatenate([halo, u], axis=0)
    gates = gate_ref[...]

    pos = i * t + lax.broadcasted_iota(jnp.int32, (t, HEAD_DIM), 0)
    pool_parts = []
    for g, win in enumerate(POOL_WINDOWS):
        sl = slice(g * HEAD_DIM, (g + 1) * HEAD_DIM)
        s = ext[:, sl]
        d = 1
        while d < win:
            s = s + pltpu.roll(s, d, 0)
            d *= 2
        count = jnp.minimum(pos + 1, win).astype(F32)
        pooled = s[HALO:, :] / count - u[:, sl]
        pool_parts.append(jnp.dot(pooled.astype(BF16), wpool_ref[g], preferred_element_type=F32))
    y_pool = jnp.concatenate(pool_parts, axis=1) * pscale_ref[...] * _silu(gates[:, WIDTH:2 * WIDTH])

    ul = ext[:, WIDTH:]
    cw = convw_ref[...]
    xc = cw[CONV_WIDTH - 1:CONV_WIDTH, :] * ul
    for j in range(1, CONV_WIDTH):
        xc = xc + cw[CONV_WIDTH - 1 - j:CONV_WIDTH - j, :] * pltpu.roll(ul, j, 0)
    xc = xc[HALO:, :] + convb_ref[...]
    xcb = xc.astype(BF16)
    bd = wrg_ref.shape[-1]
    r_pre = jnp.concatenate([jnp.dot(xcb[:, n * bd:(n + 1) * bd], wrg_ref[n], preferred_element_type=F32)
                             for n in range(WIDTH // bd)], axis=1)
    i_pre = jnp.concatenate([jnp.dot(xcb[:, n * bd:(n + 1) * bd], wig_ref[n], preferred_element_type=F32)
                             for n in range(WIDTH // bd)], axis=1)
    r = _sigmoid(r_pre + brg_ref[...])
    ig = _sigmoid(i_pre + big_ref[...])
    neg_l = -lrul_ref[...]
    softplus_neg_l = jnp.maximum(neg_l, 0.0) + jnp.log(1.0 + jnp.exp(-jnp.abs(neg_l)))
    log_a = (-LRU_C * r) * softplus_neg_l
    a = jnp.exp(log_a)
    th = jnp.tanh(log_a)
    bx = jnp.sqrt(-2.0 * th / (1.0 - th)) * (ig * xc)
    rows = lax.broadcasted_iota(jnp.int32, (t, WIDTH), 0)
    d = 1
    while d < t:
        keep = rows >= d
        bx = jnp.where(keep, a * pltpu.roll(bx, d, 0) + bx, bx)
        a = jnp.where(keep, a * pltpu.roll(a, d, 0), a)
        d *= 2
    h = bx + a * h_ref[...]
    h_ref[...] = h[t - 1:t, :]
    y_lru = h * _silu(gates[:, 2 * WIDTH:3 * WIDTH])

    kv = kv_ref[...]
    qx = qx_ref[...]
    xa_parts = []
    for hh in range(N_HEADS):
        sl = slice(hh * HEAD_DIM, (hh + 1) * HEAD_DIM)
        sc = lax.dot_general(qx[:, sl], kv[:, sl], (((1,), (1,)), ((), ())), preferred_element_type=F32)
        p = jnp.exp2(sc - jnp.max(sc, axis=-1, keepdims=True))
        denom = jnp.sum(p, axis=-1, keepdims=True)
        vh = kv[:, WIDTH + hh * HEAD_DIM:WIDTH + (hh + 1) * HEAD_DIM]
        xa_parts.append(jnp.dot(p.astype(BF16), vh, preferred_element_type=F32) / denom)
    y_xa = jnp.concatenate(xa_parts, axis=1) * _silu(gates[:, 3 * WIDTH:])

    y_sb = ysb_ref[...] * _silu(gates[:, :WIDTH])

    dm = o_ref.shape[-1]
    merged = None
    for n, y in enumerate((y_sb, y_pool, y_lru, y_xa)):
        term = _sigmoid(merge_ref[:, n * dm:(n + 1) * dm]) * jnp.dot(
            y.astype(BF16), wb_ref[n], preferred_element_type=F32)
        merged = term if merged is None else merged + term
    out = jnp.dot(merged.astype(BF16), wout_ref[...], preferred_element_type=F32)
    resid = alpha * x_ref[...] + out
    mean = jnp.mean(resid, axis=-1, keepdims=True)
    cen = resid - mean
    var = jnp.mean(cen * cen, axis=-1, keepdims=True)
    o_ref[...] = cen * lax.rsqrt(var + LN_EPS) * lng_ref[...] + lnb_ref[...]


def _mixer(pf, qkv, ysb, xf, kv, wts, batch, t, alpha):
    m, dm = xf.shape
    s = m // batch
    t = min(t, s)
    nt = s // t
    mem_len = kv.shape[0] // batch
    w_merge = N_BRANCH * dm
    gate_blk = w_merge // (4 * WIDTH)
    u_blk = (w_merge + 4 * WIDTH) // (2 * WIDTH)
    assert w_merge % (4 * WIDTH) == 0 and (w_merge + 4 * WIDTH) % (2 * WIDTH) == 0 and t % HALO == 0
    row = lambda b, i: b * nt + i
    const2 = lambda b, i: (0, 0)
    const3 = lambda b, i: (0, 0, 0)
    (wpool, pscale, convw, convb, wrg, brg, wig, big, lrul, wb, wout, lng, lnb) = wts
    in_specs = [
        pl.BlockSpec((t, w_merge), lambda b, i: (row(b, i), 0)),
        pl.BlockSpec((t, 4 * WIDTH), lambda b, i: (row(b, i), gate_blk)),
        pl.BlockSpec((t, 2 * WIDTH), lambda b, i: (row(b, i), u_blk)),
        pl.BlockSpec((HALO, 2 * WIDTH), lambda b, i: (jnp.maximum(row(b, i) * (t // HALO) - 1, 0), u_blk)),
        pl.BlockSpec((t, WIDTH), lambda b, i: (row(b, i), 3)),
        pl.BlockSpec((t, WIDTH), lambda b, i: (row(b, i), 0)),
        pl.BlockSpec((t, dm), lambda b, i: (row(b, i), 0)),
        pl.BlockSpec((mem_len, 2 * WIDTH), lambda b, i: (b, 0)),
        pl.BlockSpec(wpool.shape, const3), pl.BlockSpec(pscale.shape, const2),
        pl.BlockSpec(convw.shape, const2), pl.BlockSpec(convb.shape, const2),
        pl.BlockSpec(wrg.shape, const3), pl.BlockSpec(brg.shape, const2),
        pl.BlockSpec(wig.shape, const3), pl.BlockSpec(big.shape, const2),
        pl.BlockSpec(lrul.shape, const2),
        pl.BlockSpec(wb.shape, const3), pl.BlockSpec(wout.shape, const2),
        pl.BlockSpec(lng.shape, const2), pl.BlockSpec(lnb.shape, const2),
    ]
    return pl.pallas_call(
        functools.partial(_mixer_kernel, t=t, alpha=alpha),
        grid=(batch, nt),
        in_specs=in_specs,
        out_specs=pl.BlockSpec((t, dm), lambda b, i: (row(b, i), 0)),
        out_shape=jax.ShapeDtypeStruct((m, dm), F32),
        scratch_shapes=[pltpu.VMEM((1, WIDTH), F32)],
        compiler_params=_cparams(("arbitrary", "arbitrary")),
        name="mixer",
    )(pf, pf, pf, pf, qkv, ysb, xf, kv, wpool, pscale, convw, convb, wrg, brg, wig, big, lrul,
      wb, wout, lng, lnb)


def _block_diag_tiles(w, tile):
    h, n, _ = w.shape
    per = tile // n
    eye = jnp.eye(per, dtype=w.dtype)
    return jnp.einsum('gpij,pq->gpiqj', w.reshape(h // per, per, n, n), eye).reshape(h // per, tile, tile)


def _layer(xf, memf, batch, depth, w_in, w_pool, pool_scale, conv_w, conv_b, w_rg, b_rg, w_ig, b_ig, lru_L,
           w_mem_kv, w_branch, w_out, ln_g, ln_b, *, tq=256, t=256, tm=1024, tn=1024):
    dm = xf.shape[1]
    s = xf.shape[0] // batch
    cols = {}
    off = 0
    for name, size in (("q_sb", WIDTH), ("k_sb", WIDTH), ("v_sb", WIDTH), ("g_sb", WIDTH), ("u_pool", WIDTH),
                       ("g_pool", WIDTH), ("u_lru", WIDTH), ("g_lru", WIDTH), ("q_xa", WIDTH), ("g_xa", WIDTH),
                       ("merge", N_BRANCH * dm)):
        cols[name] = w_in[:, off:off + size]
        off += size
    qscale = LOG2E / math.sqrt(HEAD_DIM)
    w_a = jnp.concatenate([cols[n] for n in ("q_sb", "k_sb", "v_sb", "q_xa")], axis=1).astype(BF16)
    scale_a = jnp.concatenate([jnp.full((1, WIDTH), qscale, F32), jnp.ones((1, 2 * WIDTH), F32),
                               jnp.full((1, WIDTH), qscale, F32)], axis=1)
    w_b = jnp.concatenate([cols[n] for n in ("merge", "g_sb", "g_pool", "g_lru", "g_xa", "u_pool", "u_lru")],
                          axis=1).astype(BF16)
    qkv = _matmul(xf, w_a, scale_a, BF16, tm, tn)
    pf = _matmul(xf, w_b, jnp.ones((1, w_b.shape[1]), F32), F32, tm, tn)
    kv = _matmul(memf, w_mem_kv.astype(BF16), jnp.ones((1, 2 * WIDTH), F32), BF16, tm, tn)
    ysb = _sb_attention(qkv.reshape(batch, s, 4 * WIDTH), tq).reshape(batch * s, WIDTH)
    row = lambda v: v.reshape(1, -1).astype(F32)
    wts = (w_pool.astype(BF16), row(pool_scale), conv_w.astype(F32), row(conv_b),
           _block_diag_tiles(w_rg, 256).astype(BF16), row(b_rg), _block_diag_tiles(w_ig, 256).astype(BF16), row(b_ig),
           row(lru_L), w_branch.astype(BF16), w_out.astype(BF16), row(ln_g), row(ln_b))
    return _mixer(pf, qkv, ysb, xf, kv, wts, batch, t, (2 * depth) ** 0.25)


def kernel(x, mem, w_in, w_pool, pool_scale, conv_w, conv_b, w_rg, b_rg, w_ig, b_ig, lru_L, w_mem_kv, w_branch,
           w_out, ln_g, ln_b):
    b, s, dm = x.shape
    depth = w_in.shape[0]
    xf = x.reshape(b * s, dm)
    memf = mem.reshape(-1, dm)
    for l in range(depth):
        xf = _layer(xf, memf, b, depth, w_in[l], w_pool[l], pool_scale[l], conv_w[l], conv_b[l], w_rg[l], b_rg[l],
                    w_ig[l], b_ig[l], lru_L[l], w_mem_kv[l], w_branch[l], w_out[l], ln_g[l], ln_b[l])
    return xf.reshape(b, s, dm)
```

```python
import functools
import math

import jax
import jax.numpy as jnp
from jax import lax
from jax.experimental import pallas as pl
from jax.experimental.pallas import tpu as pltpu

F32 = jnp.float32
BF16 = jnp.bfloat16

HEAD_DIM = 128
N_HEADS = 4
WIDTH = N_HEADS * HEAD_DIM
N_BRANCH = 4
POOL_WINDOWS = (2, 4, 8, 16)
LRU_BLOCK_DIM = 64
LRU_C = 8.0
CONV_WIDTH = 4
LN_EPS = 1e-5
HALO = 16
LOG2E = 1.4426950408889634
SOFTPLUS_CLAMP = 64.0
VMEM_LIMIT = 56 * 1024 * 1024


def _cparams(sem):
    return pltpu.CompilerParams(dimension_semantics=sem, vmem_limit_bytes=VMEM_LIMIT)


def _mm_kernel(x_ref, w_ref, s_ref, o_ref):
    acc = jnp.dot(x_ref[...].astype(BF16), w_ref[...], preferred_element_type=F32)
    o_ref[...] = (acc * s_ref[...]).astype(o_ref.dtype)


def _matmul(x, w, scale, out_dtype, tm, tn):
    m, k = x.shape
    n = w.shape[1]
    tm, tn = min(tm, m), min(tn, n)
    return pl.pallas_call(
        _mm_kernel,
        grid=(m // tm, n // tn),
        in_specs=[pl.BlockSpec((tm, k), lambda i, j: (i, 0)),
                  pl.BlockSpec((k, tn), lambda i, j: (0, j)),
                  pl.BlockSpec((1, tn), lambda i, j: (0, j))],
        out_specs=pl.BlockSpec((tm, tn), lambda i, j: (i, j)),
        out_shape=jax.ShapeDtypeStruct((m, n), out_dtype),
        compiler_params=_cparams(("parallel", "parallel")),
        name="proj_matmul",
    )(x, w, scale)


def _sb_kernel(q_ref, k_ref, v_ref, t_ref, o_ref, acc_ref, later_ref, *, tq, unroll):
    i = pl.program_id(2)
    q = q_ref[...]
    tmat = t_ref[...]
    row = lax.broadcasted_iota(jnp.int32, (tq, tq), 0)
    col = lax.broadcasted_iota(jnp.int32, (tq, tq), 1)
    valid = col < row
    reps = tq // HEAD_DIM

    def load(j):
        start = pl.multiple_of(j * tq, tq)
        return k_ref[pl.ds(start, tq), :], v_ref[pl.ds(start, tq), :]

    def front(k, mask):
        z = lax.dot_general(q, k, (((1,), (1,)), ((), ())), preferred_element_type=F32)
        sp = jnp.maximum(z, jnp.log2(1.0 + jnp.exp2(jnp.minimum(z, SOFTPLUS_CLAMP))))
        if mask is not None:
            sp = jnp.where(mask, sp, 0.0)
        hi = sp.astype(BF16)
        lo = (sp - hi.astype(F32)).astype(BF16)
        incl = (jnp.dot(hi, tmat, preferred_element_type=F32)
                + jnp.dot(lo, tmat, preferred_element_type=F32))
        return z - incl, jnp.broadcast_to(incl[:, 0:1], (tq, HEAD_DIM))

    def back(d, later, v, mask):
        w = jnp.exp2(d + jnp.concatenate([later] * reps, axis=1))
        if mask is not None:
            w = jnp.where(mask, w, 0.0)
        return jnp.dot(w.astype(BF16), v, preferred_element_type=F32)

    k0, v0 = load(i)
    d0, tot0 = front(k0, valid)
    acc_ref[...] = back(d0, jnp.zeros((tq, HEAD_DIM), F32), v0, valid)
    later_ref[...] = -tot0

    def group(first, count):
        kvs = [load(first - g) for g in range(count)]
        fronts = [front(k, None) for k, _ in kvs]
        later = later_ref[...]
        total = None
        for (d, tot), (_, v) in zip(fronts, kvs):
            part = back(d, later, v, None)
            total = part if total is None else total + part
            later = later - tot
        acc_ref[...] += total
        later_ref[...] = later

    def groups(n, carry):
        group(i - 1 - unroll * n, unroll)
        return carry

    n_groups = i // unroll
    lax.fori_loop(0, n_groups, groups, 0)

    left = i - unroll * n_groups
    size = unroll // 2
    while size >= 1:
        @pl.when((left & size) != 0)
        def _(size=size):
            group((left & (2 * size - 1)) - 1, size)
        size //= 2

    o_ref[...] = acc_ref[...]


def _sb_attention(qkv, tq, unroll):
    b, s, _ = qkv.shape
    tq = min(tq, s)
    assert unroll & (unroll - 1) == 0 and s % tq == 0 and tq % HEAD_DIM == 0
    loc = jnp.arange(tq)
    tmat = (loc[:, None] >= loc[None, :]).astype(BF16)
    return pl.pallas_call(
        functools.partial(_sb_kernel, tq=tq, unroll=unroll),
        grid=(b, N_HEADS, s // tq),
        in_specs=[pl.BlockSpec((None, tq, HEAD_DIM), lambda bi, h, i: (bi, i, h)),
                  pl.BlockSpec((None, s, HEAD_DIM), lambda bi, h, i: (bi, 0, N_HEADS + h)),
                  pl.BlockSpec((None, s, HEAD_DIM), lambda bi, h, i: (bi, 0, 2 * N_HEADS + h)),
                  pl.BlockSpec((tq, tq), lambda bi, h, i: (0, 0))],
        out_specs=pl.BlockSpec((None, tq, HEAD_DIM), lambda bi, h, i: (bi, i, h)),
        out_shape=jax.ShapeDtypeStruct((b, s, WIDTH), F32),
        scratch_shapes=[pltpu.VMEM((tq, HEAD_DIM), F32), pltpu.VMEM((tq, HEAD_DIM), F32)],
        compiler_params=_cparams(("parallel", "parallel", "arbitrary")),
        name="stickbreak_attention",
    )(qkv, qkv, qkv, tmat)


def _sigmoid(x):
    return 1.0 / (1.0 + jnp.exp(-x))


def _silu(x):
    return x * _sigmoid(x)


def _mixer_kernel(merge_ref, gate_ref, u_ref, halo_ref, qx_ref, ysb_ref, x_ref, kv_ref,
                  wpool_ref, pscale_ref, convw_ref, convb_ref, wrg_ref, brg_ref, wig_ref, big_ref,
                  lrul_ref, wb_ref, wout_ref, lng_ref, lnb_ref, o_ref, h_ref, *, t, alpha):
    i = pl.program_id(1)

    @pl.when(i == 0)
    def _():
        h_ref[...] = jnp.zeros_like(h_ref)

    u = u_ref[...]
    halo = jnp.where(i == 0, 0.0, halo_ref[...])
    ext = jnp.concatenate([halo, u], axis=0)
    gates = gate_ref[...]

    pos = i * t + lax.broadcasted_iota(jnp.int32, (t, HEAD_DIM), 0)
    pool_parts = []
    for g, win in enumerate(POOL_WINDOWS):
        sl = slice(g * HEAD_DIM, (g + 1) * HEAD_DIM)
        s = ext[:, sl]
        d = 1
        while d < win:
            s = s + pltpu.roll(s, d, 0)
            d *= 2
        count = jnp.minimum(pos + 1, win).astype(F32)
        pooled = s[HALO:, :] / count - u[:, sl]
        pool_parts.append(jnp.dot(pooled.astype(BF16), wpool_ref[g], preferred_element_type=F32))
    y_pool = jnp.concatenate(pool_parts, axis=1) * pscale_ref[...] * _silu(gates[:, WIDTH:2 * WIDTH])

    ul = ext[:, WIDTH:]
    cw = convw_ref[...]
    xc = cw[CONV_WIDTH - 1:CONV_WIDTH, :] * ul
    for j in range(1, CONV_WIDTH):
        xc = xc + cw[CONV_WIDTH - 1 - j:CONV_WIDTH - j, :] * pltpu.roll(ul, j, 0)
    xc = xc[HALO:, :] + convb_ref[...]
    xcb = xc.astype(BF16)
    bd = wrg_ref.shape[-1]
    r_pre = jnp.concatenate([jnp.dot(xcb[:, n * bd:(n + 1) * bd], wrg_ref[n], preferred_element_type=F32)
                             for n in range(WIDTH // bd)], axis=1)
    i_pre = jnp.concatenate([jnp.dot(xcb[:, n * bd:(n + 1) * bd], wig_ref[n], preferred_element_type=F32)
                             for n in range(WIDTH // bd)], axis=1)
    r = _sigmoid(r_pre + brg_ref[...])
    ig = _sigmoid(i_pre + big_ref[...])
    neg_l = -lrul_ref[...]
    softplus_neg_l = jnp.maximum(neg_l, 0.0) + jnp.log(1.0 + jnp.exp(-jnp.abs(neg_l)))
    log_a = (-LRU_C * r) * softplus_neg_l
    a = jnp.exp(log_a)
    th = jnp.tanh(log_a)
    bx = jnp.sqrt(-2.0 * th / (1.0 - th)) * (ig * xc)
    rows = lax.broadcasted_iota(jnp.int32, (t, WIDTH), 0)
    d = 1
    while d < t:
        keep = rows >= d
        bx = jnp.where(keep, a * pltpu.roll(bx, d, 0) + bx, bx)
        a = jnp.where(keep, a * pltpu.roll(a, d, 0), a)
        d *= 2
    h = bx + a * h_ref[...]
    h_ref[...] = h[t - 1:t, :]
    y_lru = h * _silu(gates[:, 2 * WIDTH:3 * WIDTH])

    kv = kv_ref[...]
    qx = qx_ref[...]
    xa_parts = []
    for hh in range(N_HEADS):
        sl = slice(hh * HEAD_DIM, (hh + 1) * HEAD_DIM)
        sc = lax.dot_general(qx[:, sl], kv[:, sl], (((1,), (1,)), ((), ())), preferred_element_type=F32)
        p = jnp.exp2(sc - jnp.max(sc, axis=-1, keepdims=True))
        denom = jnp.sum(p, axis=-1, keepdims=True)
        vh = kv[:, WIDTH + hh * HEAD_DIM:WIDTH + (hh + 1) * HEAD_DIM]
        xa_parts.append(jnp.dot(p.astype(BF16), vh, preferred_element_type=F32) / denom)
    y_xa = jnp.concatenate(xa_parts, axis=1) * _silu(gates[:, 3 * WIDTH:])

    y_sb = ysb_ref[...] * _silu(gates[:, :WIDTH])

    dm = o_ref.shape[-1]
    merged = None
    for n, y in enumerate((y_sb, y_pool, y_lru, y_xa)):
        term = _sigmoid(merge_ref[:, n * dm:(n + 1) * dm]) * jnp.dot(
            y.astype(BF16), wb_ref[n], preferred_element_type=F32)
        merged = term if merged is None else merged + term
    out = jnp.dot(merged.astype(BF16), wout_ref[...], preferred_element_type=F32)
    resid = alpha * x_ref[...] + out
    mean = jnp.mean(resid, axis=-1, keepdims=True)
    cen = resid - mean
    var = jnp.mean(cen * cen, axis=-1, keepdims=True)
    o_ref[...] = cen * lax.rsqrt(var + LN_EPS) * lng_ref[...] + lnb_ref[...]


def _mixer(pf, qkv, ysb, xf, kv, wts, batch, t, alpha):
    m, dm = xf.shape
    s = m // batch
    t = min(t, s)
    nt = s // t
    mem_len = kv.shape[0] // batch
    w_merge = N_BRANCH * dm
    gate_blk = w_merge // (4 * WIDTH)
    u_blk = (w_merge + 4 * WIDTH) // (2 * WIDTH)
    assert w_merge % (4 * WIDTH) == 0 and (w_merge + 4 * WIDTH) % (2 * WIDTH) == 0 and t % HALO == 0
    row = lambda b, i: b * nt + i
    const2 = lambda b, i: (0, 0)
    const3 = lambda b, i: (0, 0, 0)
    (wpool, pscale, convw, convb, wrg, brg, wig, big, lrul, wb, wout, lng, lnb) = wts
    in_specs = [
        pl.BlockSpec((t, w_merge), lambda b, i: (row(b, i), 0)),
        pl.BlockSpec((t, 4 * WIDTH), lambda b, i: (row(b, i), gate_blk)),
        pl.BlockSpec((t, 2 * WIDTH), lambda b, i: (row(b, i), u_blk)),
        pl.BlockSpec((HALO, 2 * WIDTH), lambda b, i: (jnp.maximum(row(b, i) * (t // HALO) - 1, 0), u_blk)),
        pl.BlockSpec((t, WIDTH), lambda b, i: (row(b, i), 3)),
        pl.BlockSpec((t, WIDTH), lambda b, i: (row(b, i), 0)),
        pl.BlockSpec((t, dm), lambda b, i: (row(b, i), 0)),
        pl.BlockSpec((mem_len, 2 * WIDTH), lambda b, i: (b, 0)),
        pl.BlockSpec(wpool.shape, const3), pl.BlockSpec(pscale.shape, const2),
        pl.BlockSpec(convw.shape, const2), pl.BlockSpec(convb.shape, const2),
        pl.BlockSpec(wrg.shape, const3), pl.BlockSpec(brg.shape, const2),
        pl.BlockSpec(wig.shape, const3), pl.BlockSpec(big.shape, const2),
        pl.BlockSpec(lrul.shape, const2),
        pl.BlockSpec(wb.shape, const3), pl.BlockSpec(wout.shape, const2),
        pl.BlockSpec(lng.shape, const2), pl.BlockSpec(lnb.shape, const2),
    ]
    return pl.pallas_call(
        functools.partial(_mixer_kernel, t=t, alpha=alpha),
        grid=(batch, nt),
        in_specs=in_specs,
        out_specs=pl.BlockSpec((t, dm), lambda b, i: (row(b, i), 0)),
        out_shape=jax.ShapeDtypeStruct((m, dm), F32),
        scratch_shapes=[pltpu.VMEM((1, WIDTH), F32)],
        compiler_params=_cparams(("arbitrary", "arbitrary")),
        name="mixer",
    )(pf, pf, pf, pf, qkv, ysb, xf, kv, wpool, pscale, convw, convb, wrg, brg, wig, big, lrul,
      wb, wout, lng, lnb)


def _block_diag_tiles(w, tile):
    h, n, _ = w.shape
    per = tile // n
    eye = jnp.eye(per, dtype=w.dtype)
    return jnp.einsum('gpij,pq->gpiqj', w.reshape(h // per, per, n, n), eye).reshape(h // per, tile, tile)


def _layer(xf, memf, batch, depth, w_in, w_pool, pool_scale, conv_w, conv_b, w_rg, b_rg, w_ig, b_ig, lru_L,
           w_mem_kv, w_branch, w_out, ln_g, ln_b, *, tq=256, unroll=8, t=256, tm=1024, tn=1024):
    dm = xf.shape[1]
    s = xf.shape[0] // batch
    cols = {}
    off = 0
    for name, size in (("q_sb", WIDTH), ("k_sb", WIDTH), ("v_sb", WIDTH), ("g_sb", WIDTH), ("u_pool", WIDTH),
                       ("g_pool", WIDTH), ("u_lru", WIDTH), ("g_lru", WIDTH), ("q_xa", WIDTH), ("g_xa", WIDTH),
                       ("merge", N_BRANCH * dm)):
        cols[name] = w_in[:, off:off + size]
        off += size
    qscale = LOG2E / math.sqrt(HEAD_DIM)
    w_a = jnp.concatenate([cols[n] for n in ("q_sb", "k_sb", "v_sb", "q_xa")], axis=1).astype(BF16)
    scale_a = jnp.concatenate([jnp.full((1, WIDTH), qscale, F32), jnp.ones((1, 2 * WIDTH), F32),
                               jnp.full((1, WIDTH), qscale, F32)], axis=1)
    w_b = jnp.concatenate([cols[n] for n in ("merge", "g_sb", "g_pool", "g_lru", "g_xa", "u_pool", "u_lru")],
                          axis=1).astype(BF16)
    qkv = _matmul(xf, w_a, scale_a, BF16, tm, tn)
    pf = _matmul(xf, w_b, jnp.ones((1, w_b.shape[1]), F32), F32, tm, tn)
    kv = _matmul(memf, w_mem_kv.astype(BF16), jnp.ones((1, 2 * WIDTH), F32), BF16, tm, tn)
    ysb = _sb_attention(qkv.reshape(batch, s, 4 * WIDTH), tq, unroll).reshape(batch * s, WIDTH)
    row = lambda v: v.reshape(1, -1).astype(F32)
    wts = (w_pool.astype(BF16), row(pool_scale), conv_w.astype(F32), row(conv_b),
           _block_diag_tiles(w_rg, 256).astype(BF16), row(b_rg), _block_diag_tiles(w_ig, 256).astype(BF16), row(b_ig),
           row(lru_L), w_branch.astype(BF16), w_out.astype(BF16), row(ln_g), row(ln_b))
    return _mixer(pf, qkv, ysb, xf, kv, wts, batch, t, (2 * depth) ** 0.25)


def kernel(x, mem, w_in, w_pool, pool_scale, conv_w, conv_b, w_rg, b_rg, w_ig, b_ig, lru_L, w_mem_kv, w_branch,
           w_out, ln_g, ln_b):
    b, s, dm = x.shape
    depth = w_in.shape[0]
    xf = x.reshape(b * s, dm)
    memf = mem.reshape(-1, dm)
    for l in range(depth):
        xf = _layer(xf, memf, b, depth, w_in[l], w_pool[l], pool_scale[l], conv_w[l], conv_b[l], w_rg[l], b_rg[l],
                    w_ig[l], b_ig[l], lru_L[l], w_mem_kv[l], w_branch[l], w_out[l], ln_g[l], ln_b[l])
    return xf.reshape(b, s, dm)
```

```python
import functools
import math

import jax
import jax.numpy as jnp
from jax import lax
from jax.experimental import pallas as pl
from jax.experimental.pallas import tpu as pltpu

F32 = jnp.float32
BF16 = jnp.bfloat16

HEAD_DIM = 128
N_HEADS = 4
WIDTH = N_HEADS * HEAD_DIM
N_BRANCH = 4
POOL_WINDOWS = (2, 4, 8, 16)
LRU_BLOCK_DIM = 64
LRU_C = 8.0
CONV_WIDTH = 4
LN_EPS = 1e-5
SUBLANES = 8
HALO = 16
LOG2E = 1.4426950408889634
SOFTPLUS_CLAMP = 64.0
VMEM_LIMIT = 56 * 1024 * 1024


def _cparams(sem):
    return pltpu.CompilerParams(dimension_semantics=sem, vmem_limit_bytes=VMEM_LIMIT)


def _mm_kernel(x_ref, w_ref, s_ref, o_ref):
    acc = jnp.dot(x_ref[...].astype(BF16), w_ref[...], preferred_element_type=F32)
    o_ref[...] = (acc * s_ref[...]).astype(o_ref.dtype)


def _matmul(x, w, scale, out_dtype, tm, tn):
    m, k = x.shape
    n = w.shape[1]
    tm, tn = min(tm, m), min(tn, n)
    return pl.pallas_call(
        _mm_kernel,
        grid=(m // tm, n // tn),
        in_specs=[pl.BlockSpec((tm, k), lambda i, j: (i, 0)),
                  pl.BlockSpec((k, tn), lambda i, j: (0, j)),
                  pl.BlockSpec((1, tn), lambda i, j: (0, j))],
        out_specs=pl.BlockSpec((tm, tn), lambda i, j: (i, j)),
        out_shape=jax.ShapeDtypeStruct((m, n), out_dtype),
        compiler_params=_cparams(("parallel", "parallel")),
        name="proj_matmul",
    )(x, w, scale)


def _sb_kernel(q_ref, k_ref, v_ref, t_ref, o_ref, acc_ref, later_ref, d_ref, tot_ref, *, tq, group):
    i = pl.program_id(2)
    q = q_ref[...]
    tmat = t_ref[...]
    row = lax.broadcasted_iota(jnp.int32, (tq, tq), 0)
    col = lax.broadcasted_iota(jnp.int32, (tq, tq), 1)
    valid = col < row
    reps = tq // HEAD_DIM

    def rows(j):
        return pl.ds(pl.multiple_of(j * tq, tq), tq)

    def scores(j):
        return lax.dot_general(q, k_ref[rows(j), :], (((1,), (1,)), ((), ())), preferred_element_type=F32)

    def front(z, mask):
        sp = jnp.maximum(z, jnp.log2(1.0 + jnp.exp2(jnp.minimum(z, SOFTPLUS_CLAMP))))
        if mask is not None:
            sp = jnp.where(mask, sp, 0.0)
        incl = jnp.dot(sp.astype(BF16), tmat, preferred_element_type=F32)
        return z - incl, jnp.broadcast_to(incl[:, 0:1], (tq, HEAD_DIM))

    def back(d, later, v, mask):
        w = jnp.exp2(d + jnp.concatenate([later] * reps, axis=1))
        if mask is not None:
            w = jnp.where(mask, w, 0.0)
        return jnp.dot(w.astype(BF16), v, preferred_element_type=F32)

    def tile_of(m):
        return jnp.maximum(i - 1 - m, 0)

    def fronts_to_scratch(g):
        for t in range(group):
            d_ref[t], tot_ref[t] = front(scores(tile_of(g * group + t)), None)

    def backs_from_scratch(g):
        later = later_ref[...]
        total = None
        for t in range(group):
            m = g * group + t
            v = v_ref[rows(tile_of(m)), :]
            part = back(d_ref[t], later, jnp.where(m < i, v, jnp.zeros_like(v)), None)
            total = part if total is None else total + part
            later = later - tot_ref[t]
        acc_ref[...] += total
        later_ref[...] = later

    d0, tot0 = front(scores(i), valid)
    acc_ref[...] = back(d0, jnp.zeros((tq, HEAD_DIM), F32), v_ref[rows(i), :], valid)
    later_ref[...] = -tot0
    fronts_to_scratch(0)
    n_groups = (i + group - 1) // group

    def trip(g, carry):
        backs_from_scratch(g - 1)
        fronts_to_scratch(g)
        return carry

    lax.fori_loop(1, n_groups, trip, 0)

    @pl.when(n_groups > 0)
    def _():
        backs_from_scratch(n_groups - 1)

    o_ref[...] = acc_ref[...]


def _sb_attention(qkv, tq, group):
    b, s, _ = qkv.shape
    tq = min(tq, s)
    assert s % tq == 0 and tq % HEAD_DIM == 0
    loc = jnp.arange(tq)
    tmat = (loc[:, None] >= loc[None, :]).astype(BF16)
    return pl.pallas_call(
        functools.partial(_sb_kernel, tq=tq, group=group),
        grid=(b, N_HEADS, s // tq),
        in_specs=[pl.BlockSpec((None, tq, HEAD_DIM), lambda bi, h, i: (bi, i, h)),
                  pl.BlockSpec((None, s, HEAD_DIM), lambda bi, h, i: (bi, 0, N_HEADS + h)),
                  pl.BlockSpec((None, s, HEAD_DIM), lambda bi, h, i: (bi, 0, 2 * N_HEADS + h)),
                  pl.BlockSpec((tq, tq), lambda bi, h, i: (0, 0))],
        out_specs=pl.BlockSpec((None, tq, HEAD_DIM), lambda bi, h, i: (bi, i, h)),
        out_shape=jax.ShapeDtypeStruct((b, s, WIDTH), F32),
        scratch_shapes=[pltpu.VMEM((tq, HEAD_DIM), F32), pltpu.VMEM((tq, HEAD_DIM), F32),
                        pltpu.VMEM((group, tq, tq), F32), pltpu.VMEM((group, tq, HEAD_DIM), F32)],
        compiler_params=_cparams(("parallel", "parallel", "arbitrary")),
        name="stickbreak_attention",
    )(qkv, qkv, qkv, tmat)


def _sigmoid(x):
    return 0.5 + 0.5 * jnp.tanh(0.5 * x)


def _silu(x):
    half = 0.5 * x
    return half + half * jnp.tanh(half)


def _mixer_kernel(merge_ref, gate_ref, u_ref, halo_ref, qx_ref, ysb_ref, x_ref, kv_ref,
                  wpool_ref, pscale_ref, convw_ref, convb_ref, wrg_ref, brg_ref, wig_ref, big_ref,
                  lrul_ref, wb_ref, wout_ref, lng_ref, lnb_ref, o_ref, h_ref, *, t, alpha):
    i = pl.program_id(1)

    @pl.when(i == 0)
    def _():
        h_ref[...] = jnp.zeros_like(h_ref)

    u = u_ref[...]
    halo = jnp.where(i == 0, 0.0, halo_ref[...])
    ext = jnp.concatenate([halo, u], axis=0)
    gates = gate_ref[...]

    pos = i * t + lax.broadcasted_iota(jnp.int32, (t, HEAD_DIM), 0)
    pool_parts = []
    for g, win in enumerate(POOL_WINDOWS):
        sl = slice(g * HEAD_DIM, (g + 1) * HEAD_DIM)
        s = ext[:, sl]
        d = 1
        while d < win:
            s = s + pltpu.roll(s, d, 0)
            d *= 2
        count = jnp.minimum(pos + 1, win).astype(F32)
        pooled = s[HALO:, :] / count - u[:, sl]
        pool_parts.append(jnp.dot(pooled.astype(BF16), wpool_ref[g], preferred_element_type=F32))
    y_pool = jnp.concatenate(pool_parts, axis=1) * pscale_ref[...] * _silu(gates[:, WIDTH:2 * WIDTH])

    ul = ext[:, WIDTH:]
    cw = convw_ref[...]
    xc = cw[CONV_WIDTH - 1:CONV_WIDTH, :] * ul
    for j in range(1, CONV_WIDTH):
        xc = xc + cw[CONV_WIDTH - 1 - j:CONV_WIDTH - j, :] * pltpu.roll(ul, j, 0)
    xc = xc[HALO:, :] + convb_ref[...]
    xcb = xc.astype(BF16)
    bd = wrg_ref.shape[-1]
    r_pre = jnp.concatenate([jnp.dot(xcb[:, n * bd:(n + 1) * bd], wrg_ref[n], preferred_element_type=F32)
                             for n in range(WIDTH // bd)], axis=1)
    i_pre = jnp.concatenate([jnp.dot(xcb[:, n * bd:(n + 1) * bd], wig_ref[n], preferred_element_type=F32)
                             for n in range(WIDTH // bd)], axis=1)
    r = _sigmoid(r_pre + brg_ref[...])
    ig = _sigmoid(i_pre + big_ref[...])
    neg_l = -lrul_ref[...]
    softplus_neg_l = jnp.maximum(neg_l, 0.0) + jnp.log(1.0 + jnp.exp(-jnp.abs(neg_l)))
    log_a = (-LRU_C * r) * softplus_neg_l
    a = jnp.exp(log_a)
    th = jnp.tanh(log_a)
    bx = jnp.sqrt(-2.0 * th / (1.0 - th)) * (ig * xc)
    in_group = lax.broadcasted_iota(jnp.int32, (t, WIDTH), 0) % SUBLANES
    d = 1
    while d < SUBLANES:
        keep = in_group >= d
        bx = jnp.where(keep, a * pltpu.roll(bx, d, 0) + bx, bx)
        a = jnp.where(keep, a * pltpu.roll(a, d, 0), a)
        d *= 2
    carry = h_ref[...]
    pieces = []
    for g in range(t // SUBLANES):
        sl = slice(g * SUBLANES, (g + 1) * SUBLANES)
        piece = bx[sl, :] + a[sl, :] * carry
        pieces.append(piece)
        carry = piece[SUBLANES - 1:SUBLANES, :]
    h = jnp.concatenate(pieces, axis=0)
    h_ref[...] = carry
    y_lru = h * _silu(gates[:, 2 * WIDTH:3 * WIDTH])

    kv = kv_ref[...]
    qx = qx_ref[...]
    xa_parts = []
    for hh in range(N_HEADS):
        sl = slice(hh * HEAD_DIM, (hh + 1) * HEAD_DIM)
        sc = lax.dot_general(qx[:, sl], kv[:, sl], (((1,), (1,)), ((), ())), preferred_element_type=F32)
        p = jnp.exp2(sc - jnp.max(sc, axis=-1, keepdims=True))
        denom = jnp.sum(p, axis=-1, keepdims=True)
        vh = kv[:, WIDTH + hh * HEAD_DIM:WIDTH + (hh + 1) * HEAD_DIM]
        xa_parts.append(jnp.dot(p.astype(BF16), vh, preferred_element_type=F32) / denom)
    y_xa = jnp.concatenate(xa_parts, axis=1) * _silu(gates[:, 3 * WIDTH:])

    y_sb = ysb_ref[...] * _silu(gates[:, :WIDTH])

    dm = o_ref.shape[-1]
    merged = None
    for n, y in enumerate((y_sb, y_pool, y_lru, y_xa)):
        term = _sigmoid(merge_ref[:, n * dm:(n + 1) * dm]) * jnp.dot(
            y.astype(BF16), wb_ref[n], preferred_element_type=F32)
        merged = term if merged is None else merged + term
    out = jnp.dot(merged.astype(BF16), wout_ref[...], preferred_element_type=F32)
    resid = alpha * x_ref[...] + out
    mean = jnp.mean(resid, axis=-1, keepdims=True)
    cen = resid - mean
    var = jnp.mean(cen * cen, axis=-1, keepdims=True)
    o_ref[...] = cen * lax.rsqrt(var + LN_EPS) * lng_ref[...] + lnb_ref[...]


def _mixer(pf, qkv, ysb, xf, kv, wts, batch, t, alpha):
    m, dm = xf.shape
    s = m // batch
    t = min(t, s)
    nt = s // t
    mem_len = kv.shape[0] // batch
    w_merge = N_BRANCH * dm
    gate_blk = w_merge // (4 * WIDTH)
    u_blk = (w_merge + 4 * WIDTH) // (2 * WIDTH)
    assert w_merge % (4 * WIDTH) == 0 and (w_merge + 4 * WIDTH) % (2 * WIDTH) == 0 and t % HALO == 0
    row = lambda b, i: b * nt + i
    const2 = lambda b, i: (0, 0)
    const3 = lambda b, i: (0, 0, 0)
    (wpool, pscale, convw, convb, wrg, brg, wig, big, lrul, wb, wout, lng, lnb) = wts
    in_specs = [
        pl.BlockSpec((t, w_merge), lambda b, i: (row(b, i), 0)),
        pl.BlockSpec((t, 4 * WIDTH), lambda b, i: (row(b, i), gate_blk)),
        pl.BlockSpec((t, 2 * WIDTH), lambda b, i: (row(b, i), u_blk)),
        pl.BlockSpec((HALO, 2 * WIDTH), lambda b, i: (jnp.maximum(row(b, i) * (t // HALO) - 1, 0), u_blk)),
        pl.BlockSpec((t, WIDTH), lambda b, i: (row(b, i), 3)),
        pl.BlockSpec((t, WIDTH), lambda b, i: (row(b, i), 0)),
        pl.BlockSpec((t, dm), lambda b, i: (row(b, i), 0)),
        pl.BlockSpec((mem_len, 2 * WIDTH), lambda b, i: (b, 0)),
        pl.BlockSpec(wpool.shape, const3), pl.BlockSpec(pscale.shape, const2),
        pl.BlockSpec(convw.shape, const2), pl.BlockSpec(convb.shape, const2),
        pl.BlockSpec(wrg.shape, const3), pl.BlockSpec(brg.shape, const2),
        pl.BlockSpec(wig.shape, const3), pl.BlockSpec(big.shape, const2),
        pl.BlockSpec(lrul.shape, const2),
        pl.BlockSpec(wb.shape, const3), pl.BlockSpec(wout.shape, const2),
        pl.BlockSpec(lng.shape, const2), pl.BlockSpec(lnb.shape, const2),
    ]
    return pl.pallas_call(
        functools.partial(_mixer_kernel, t=t, alpha=alpha),
        grid=(batch, nt),
        in_specs=in_specs,
        out_specs=pl.BlockSpec((t, dm), lambda b, i: (row(b, i), 0)),
        out_shape=jax.ShapeDtypeStruct((m, dm), F32),
        scratch_shapes=[pltpu.VMEM((1, WIDTH), F32)],
        compiler_params=_cparams(("arbitrary", "arbitrary")),
        name="mixer",
    )(pf, pf, pf, pf, qkv, ysb, xf, kv, wpool, pscale, convw, convb, wrg, brg, wig, big, lrul,
      wb, wout, lng, lnb)


def _block_diag_tiles(w, tile):
    h, n, _ = w.shape
    per = tile // n
    eye = jnp.eye(per, dtype=w.dtype)
    return jnp.einsum('gpij,pq->gpiqj', w.reshape(h // per, per, n, n), eye).reshape(h // per, tile, tile)


def _layer(xf, memf, batch, depth, w_in, w_pool, pool_scale, conv_w, conv_b, w_rg, b_rg, w_ig, b_ig, lru_L,
           w_mem_kv, w_branch, w_out, ln_g, ln_b, *, tq=256, group=8, t=256, tm=1024, tn=1024):
    dm = xf.shape[1]
    s = xf.shape[0] // batch
    cols = {}
    off = 0
    for name, size in (("q_sb", WIDTH), ("k_sb", WIDTH), ("v_sb", WIDTH), ("g_sb", WIDTH), ("u_pool", WIDTH),
                       ("g_pool", WIDTH), ("u_lru", WIDTH), ("g_lru", WIDTH), ("q_xa", WIDTH), ("g_xa", WIDTH),
                       ("merge", N_BRANCH * dm)):
        cols[name] = w_in[:, off:off + size]
        off += size
    qscale = LOG2E / math.sqrt(HEAD_DIM)
    w_a = jnp.concatenate([cols[n] for n in ("q_sb", "k_sb", "v_sb", "q_xa")], axis=1).astype(BF16)
    scale_a = jnp.concatenate([jnp.full((1, WIDTH), qscale, F32), jnp.ones((1, 2 * WIDTH), F32),
                               jnp.full((1, WIDTH), qscale, F32)], axis=1)
    w_b = jnp.concatenate([cols[n] for n in ("merge", "g_sb", "g_pool", "g_lru", "g_xa", "u_pool", "u_lru")],
                          axis=1).astype(BF16)
    qkv = _matmul(xf, w_a, scale_a, BF16, tm, tn)
    pf = _matmul(xf, w_b, jnp.ones((1, w_b.shape[1]), F32), F32, tm, tn)
    kv = _matmul(memf, w_mem_kv.astype(BF16), jnp.ones((1, 2 * WIDTH), F32), BF16, tm, tn)
    ysb = _sb_attention(qkv.reshape(batch, s, 4 * WIDTH), tq, group).reshape(batch * s, WIDTH)
    row = lambda v: v.reshape(1, -1).astype(F32)
    wts = (w_pool.astype(BF16), row(pool_scale), conv_w.astype(F32), row(conv_b),
           _block_diag_tiles(w_rg, 256).astype(BF16), row(b_rg), _block_diag_tiles(w_ig, 256).astype(BF16), row(b_ig),
           row(lru_L), w_branch.astype(BF16), w_out.astype(BF16), row(ln_g), row(ln_b))
    return _mixer(pf, qkv, ysb, xf, kv, wts, batch, t, (2 * depth) ** 0.25)


def kernel(x, mem, w_in, w_pool, pool_scale, conv_w, conv_b, w_rg, b_rg, w_ig, b_ig, lru_L, w_mem_kv, w_branch,
           w_out, ln_g, ln_b):
    b, s, dm = x.shape
    depth = w_in.shape[0]
    xf = x.reshape(b * s, dm)
    memf = mem.reshape(-1, dm)
    for l in range(depth):
        xf = _layer(xf, memf, b, depth, w_in[l], w_pool[l], pool_scale[l], conv_w[l], conv_b[l], w_rg[l], b_rg[l],
                    w_ig[l], b_ig[l], lru_L[l], w_mem_kv[l], w_branch[l], w_out[l], ln_g[l], ln_b[l])
    return xf.reshape(b, s, dm)
```

```python
import functools
import math

import jax
import jax.numpy as jnp
from jax import lax
from jax.experimental import pallas as pl
from jax.experimental.pallas import tpu as pltpu

F32 = jnp.float32
BF16 = jnp.bfloat16

HEAD_DIM = 128
N_HEADS = 4
WIDTH = N_HEADS * HEAD_DIM
N_BRANCH = 4
POOL_WINDOWS = (2, 4, 8, 16)
LRU_BLOCK_DIM = 64
LRU_C = 8.0
CONV_WIDTH = 4
LN_EPS = 1e-5
SUBLANES = 8
HALO = 16
LOG2E = 1.4426950408889634
SOFTPLUS_CLAMP = 64.0
STICK_EXHAUSTED_BITS = 192.0
VMEM_LIMIT = 56 * 1024 * 1024


def _cparams(sem):
    return pltpu.CompilerParams(dimension_semantics=sem, vmem_limit_bytes=VMEM_LIMIT)


def _mm_kernel(x_ref, w_ref, s_ref, o_ref):
    acc = jnp.dot(x_ref[...].astype(BF16), w_ref[...], preferred_element_type=F32)
    o_ref[...] = (acc * s_ref[...]).astype(o_ref.dtype)


def _matmul(x, w, scale, out_dtype, tm, tn):
    m, k = x.shape
    n = w.shape[1]
    tm, tn = min(tm, m), min(tn, n)
    return pl.pallas_call(
        _mm_kernel,
        grid=(m // tm, n // tn),
        in_specs=[pl.BlockSpec((tm, k), lambda i, j: (i, 0)),
                  pl.BlockSpec((k, tn), lambda i, j: (0, j)),
                  pl.BlockSpec((1, tn), lambda i, j: (0, j))],
        out_specs=pl.BlockSpec((tm, tn), lambda i, j: (i, j)),
        out_shape=jax.ShapeDtypeStruct((m, n), out_dtype),
        compiler_params=_cparams(("parallel", "parallel")),
        name="proj_matmul",
    )(x, w, scale)


def _sb_kernel(q_ref, k_ref, v_ref, t_ref, o_ref, acc_ref, later_ref, d_ref, tot_ref, *, tq, group):
    i = pl.program_id(2)
    q = q_ref[...]
    tmat = t_ref[...]
    row = lax.broadcasted_iota(jnp.int32, (tq, tq), 0)
    col = lax.broadcasted_iota(jnp.int32, (tq, tq), 1)
    valid = col < row
    reps = tq // HEAD_DIM

    def rows(j):
        return pl.ds(pl.multiple_of(j * tq, tq), tq)

    def scores(j):
        return lax.dot_general(q, k_ref[rows(j), :], (((1,), (1,)), ((), ())), preferred_element_type=F32)

    def front(z, mask):
        sp = jnp.maximum(z, jnp.log2(1.0 + jnp.exp2(jnp.minimum(z, SOFTPLUS_CLAMP))))
        if mask is not None:
            sp = jnp.where(mask, sp, 0.0)
        incl = jnp.dot(sp.astype(BF16), tmat, preferred_element_type=F32)
        return z - incl, jnp.broadcast_to(incl[:, 0:1], (tq, HEAD_DIM))

    def back(d, later, v, mask):
        w = jnp.exp2(d + jnp.concatenate([later] * reps, axis=1))
        if mask is not None:
            w = jnp.where(mask, w, 0.0)
        return jnp.dot(w.astype(BF16), v, preferred_element_type=F32)

    def tile_of(m):
        return jnp.maximum(i - 1 - m, 0)

    def masked_v(m):
        v = v_ref[rows(tile_of(m)), :]
        return jnp.where(m < i, v, jnp.zeros_like(v))

    d0, tot0 = front(scores(i), valid)
    later0 = -tot0
    d1, tot1 = front(scores(tile_of(0)), None)
    acc_ref[...] = (back(d0, jnp.zeros((tq, HEAD_DIM), F32), v_ref[rows(i), :], valid)
                    + back(d1, later0, masked_v(0), None))
    later_ref[...] = later0 - tot1

    def stick_left():
        return (jnp.max(later_ref[...]) > -STICK_EXHAUSTED_BITS).astype(jnp.int32)

    def fronts_to_scratch(g):
        for t in range(group):
            d_ref[t], tot_ref[t] = front(scores(tile_of(1 + g * group + t)), None)

    def backs_from_scratch(g):
        later = later_ref[...]
        total = None
        for t in range(group):
            part = back(d_ref[t], later, masked_v(1 + g * group + t), None)
            total = part if total is None else total + part
            later = later - tot_ref[t]
        acc_ref[...] += total
        later_ref[...] = later

    n_groups = (jnp.maximum(i - 1, 0) + group - 1) // group
    started = jnp.where(n_groups > 0, stick_left(), 0)

    @pl.when(started == 1)
    def _():
        fronts_to_scratch(0)

    def trip(state):
        g, _ = state
        backs_from_scratch(g)
        fronts_to_scratch(g + 1)
        return g + 1, stick_left()

    last, live = lax.while_loop(lambda state: (state[1] == 1) & (state[0] + 1 < n_groups), trip,
                                (jnp.int32(0), started))

    @pl.when(live == 1)
    def _():
        backs_from_scratch(last)

    o_ref[...] = acc_ref[...]


def _sb_attention(qkv, tq, group):
    b, s, _ = qkv.shape
    tq = min(tq, s)
    assert s % tq == 0 and tq % HEAD_DIM == 0
    loc = jnp.arange(tq)
    tmat = (loc[:, None] >= loc[None, :]).astype(BF16)
    return pl.pallas_call(
        functools.partial(_sb_kernel, tq=tq, group=group),
        grid=(b, N_HEADS, s // tq),
        in_specs=[pl.BlockSpec((None, tq, HEAD_DIM), lambda bi, h, i: (bi, i, h)),
                  pl.BlockSpec((None, s, HEAD_DIM), lambda bi, h, i: (bi, 0, N_HEADS + h)),
                  pl.BlockSpec((None, s, HEAD_DIM), lambda bi, h, i: (bi, 0, 2 * N_HEADS + h)),
                  pl.BlockSpec((tq, tq), lambda bi, h, i: (0, 0))],
        out_specs=pl.BlockSpec((None, tq, HEAD_DIM), lambda bi, h, i: (bi, i, h)),
        out_shape=jax.ShapeDtypeStruct((b, s, WIDTH), F32),
        scratch_shapes=[pltpu.VMEM((tq, HEAD_DIM), F32), pltpu.VMEM((tq, HEAD_DIM), F32),
                        pltpu.VMEM((group, tq, tq), F32), pltpu.VMEM((group, tq, HEAD_DIM), F32)],
        compiler_params=_cparams(("parallel", "parallel", "arbitrary")),
        name="stickbreak_attention",
    )(qkv, qkv, qkv, tmat)


def _sigmoid(x):
    return 0.5 + 0.5 * jnp.tanh(0.5 * x)


def _silu(x):
    half = 0.5 * x
    return half + half * jnp.tanh(half)


def _mixer_kernel(merge_ref, gate_ref, u_ref, halo_ref, qx_ref, ysb_ref, x_ref, kv_ref,
                  wpool_ref, pscale_ref, convw_ref, convb_ref, wrg_ref, brg_ref, wig_ref, big_ref,
                  lrul_ref, wb_ref, wout_ref, lng_ref, lnb_ref, o_ref, h_ref, *, t, alpha):
    i = pl.program_id(1)

    @pl.when(i == 0)
    def _():
        h_ref[...] = jnp.zeros_like(h_ref)

    u = u_ref[...]
    halo = jnp.where(i == 0, 0.0, halo_ref[...])
    ext = jnp.concatenate([halo, u], axis=0)
    gates = gate_ref[...]

    pos = i * t + lax.broadcasted_iota(jnp.int32, (t, HEAD_DIM), 0)
    pool_parts = []
    for g, win in enumerate(POOL_WINDOWS):
        sl = slice(g * HEAD_DIM, (g + 1) * HEAD_DIM)
        s = ext[:, sl]
        d = 1
        while d < win:
            s = s + pltpu.roll(s, d, 0)
            d *= 2
        count = jnp.minimum(pos + 1, win).astype(F32)
        pooled = s[HALO:, :] / count - u[:, sl]
        pool_parts.append(jnp.dot(pooled.astype(BF16), wpool_ref[g], preferred_element_type=F32))
    y_pool = jnp.concatenate(pool_parts, axis=1) * pscale_ref[...] * _silu(gates[:, WIDTH:2 * WIDTH])

    ul = ext[:, WIDTH:]
    cw = convw_ref[...]
    xc = cw[CONV_WIDTH - 1:CONV_WIDTH, :] * ul
    for j in range(1, CONV_WIDTH):
        xc = xc + cw[CONV_WIDTH - 1 - j:CONV_WIDTH - j, :] * pltpu.roll(ul, j, 0)
    xc = xc[HALO:, :] + convb_ref[...]
    xcb = xc.astype(BF16)
    bd = wrg_ref.shape[-1]
    r_pre = jnp.concatenate([jnp.dot(xcb[:, n * bd:(n + 1) * bd], wrg_ref[n], preferred_element_type=F32)
                             for n in range(WIDTH // bd)], axis=1)
    i_pre = jnp.concatenate([jnp.dot(xcb[:, n * bd:(n + 1) * bd], wig_ref[n], preferred_element_type=F32)
                             for n in range(WIDTH // bd)], axis=1)
    r = _sigmoid(r_pre + brg_ref[...])
    ig = _sigmoid(i_pre + big_ref[...])
    neg_l = -lrul_ref[...]
    softplus_neg_l = jnp.maximum(neg_l, 0.0) + jnp.log(1.0 + jnp.exp(-jnp.abs(neg_l)))
    log_a = (-LRU_C * r) * softplus_neg_l
    a = jnp.exp(log_a)
    th = jnp.tanh(log_a)
    bx = jnp.sqrt(-2.0 * th / (1.0 - th)) * (ig * xc)
    in_group = lax.broadcasted_iota(jnp.int32, (t, WIDTH), 0) % SUBLANES
    d = 1
    while d < SUBLANES:
        keep = in_group >= d
        bx = jnp.where(keep, a * pltpu.roll(bx, d, 0) + bx, bx)
        a = jnp.where(keep, a * pltpu.roll(a, d, 0), a)
        d *= 2
    carry = h_ref[...]
    pieces = []
    for g in range(t // SUBLANES):
        sl = slice(g * SUBLANES, (g + 1) * SUBLANES)
        piece = bx[sl, :] + a[sl, :] * carry
        pieces.append(piece)
        carry = piece[SUBLANES - 1:SUBLANES, :]
    h = jnp.concatenate(pieces, axis=0)
    h_ref[...] = carry
    y_lru = h * _silu(gates[:, 2 * WIDTH:3 * WIDTH])

    kv = kv_ref[...]
    qx = qx_ref[...]
    xa_parts = []
    for hh in range(N_HEADS):
        sl = slice(hh * HEAD_DIM, (hh + 1) * HEAD_DIM)
        sc = lax.dot_general(qx[:, sl], kv[:, sl], (((1,), (1,)), ((), ())), preferred_element_type=F32)
        p = jnp.exp2(sc - jnp.max(sc, axis=-1, keepdims=True))
        denom = jnp.sum(p, axis=-1, keepdims=True)
        vh = kv[:, WIDTH + hh * HEAD_DIM:WIDTH + (hh + 1) * HEAD_DIM]
        xa_parts.append(jnp.dot(p.astype(BF16), vh, preferred_element_type=F32) / denom)
    y_xa = jnp.concatenate(xa_parts, axis=1) * _silu(gates[:, 3 * WIDTH:])

    y_sb = ysb_ref[...] * _silu(gates[:, :WIDTH])

    dm = o_ref.shape[-1]
    merged = None
    for n, y in enumerate((y_sb, y_pool, y_lru, y_xa)):
        term = _sigmoid(merge_ref[:, n * dm:(n + 1) * dm]) * jnp.dot(
            y.astype(BF16), wb_ref[n], preferred_element_type=F32)
        merged = term if merged is None else merged + term
    out = jnp.dot(merged.astype(BF16), wout_ref[...], preferred_element_type=F32)
    resid = alpha * x_ref[...] + out
    mean = jnp.mean(resid, axis=-1, keepdims=True)
    cen = resid - mean
    var = jnp.mean(cen * cen, axis=-1, keepdims=True)
    o_ref[...] = cen * lax.rsqrt(var + LN_EPS) * lng_ref[...] + lnb_ref[...]


def _mixer(pf, qkv, ysb, xf, kv, wts, batch, t, alpha):
    m, dm = xf.shape
    s = m // batch
    t = min(t, s)
    nt = s // t
    mem_len = kv.shape[0] // batch
    w_merge = N_BRANCH * dm
    gate_blk = w_merge // (4 * WIDTH)
    u_blk = (w_merge + 4 * WIDTH) // (2 * WIDTH)
    assert w_merge % (4 * WIDTH) == 0 and (w_merge + 4 * WIDTH) % (2 * WIDTH) == 0 and t % HALO == 0
    row = lambda b, i: b * nt + i
    const2 = lambda b, i: (0, 0)
    const3 = lambda b, i: (0, 0, 0)
    (wpool, pscale, convw, convb, wrg, brg, wig, big, lrul, wb, wout, lng, lnb) = wts
    in_specs = [
        pl.BlockSpec((t, w_merge), lambda b, i: (row(b, i), 0)),
        pl.BlockSpec((t, 4 * WIDTH), lambda b, i: (row(b, i), gate_blk)),
        pl.BlockSpec((t, 2 * WIDTH), lambda b, i: (row(b, i), u_blk)),
        pl.BlockSpec((HALO, 2 * WIDTH), lambda b, i: (jnp.maximum(row(b, i) * (t // HALO) - 1, 0), u_blk)),
        pl.BlockSpec((t, WIDTH), lambda b, i: (row(b, i), 3)),
        pl.BlockSpec((t, WIDTH), lambda b, i: (row(b, i), 0)),
        pl.BlockSpec((t, dm), lambda b, i: (row(b, i), 0)),
        pl.BlockSpec((mem_len, 2 * WIDTH), lambda b, i: (b, 0)),
        pl.BlockSpec(wpool.shape, const3), pl.BlockSpec(pscale.shape, const2),
        pl.BlockSpec(convw.shape, const2), pl.BlockSpec(convb.shape, const2),
        pl.BlockSpec(wrg.shape, const3), pl.BlockSpec(brg.shape, const2),
        pl.BlockSpec(wig.shape, const3), pl.BlockSpec(big.shape, const2),
        pl.BlockSpec(lrul.shape, const2),
        pl.BlockSpec(wb.shape, const3), pl.BlockSpec(wout.shape, const2),
        pl.BlockSpec(lng.shape, const2), pl.BlockSpec(lnb.shape, const2),
    ]
    return pl.pallas_call(
        functools.partial(_mixer_kernel, t=t, alpha=alpha),
        grid=(batch, nt),
        in_specs=in_specs,
        out_specs=pl.BlockSpec((t, dm), lambda b, i: (row(b, i), 0)),
        out_shape=jax.ShapeDtypeStruct((m, dm), F32),
        scratch_shapes=[pltpu.VMEM((1, WIDTH), F32)],
        compiler_params=_cparams(("arbitrary", "arbitrary")),
        name="mixer",
    )(pf, pf, pf, pf, qkv, ysb, xf, kv, wpool, pscale, convw, convb, wrg, brg, wig, big, lrul,
      wb, wout, lng, lnb)


def _block_diag_tiles(w, tile):
    h, n, _ = w.shape
    per = tile // n
    eye = jnp.eye(per, dtype=w.dtype)
    return jnp.einsum('gpij,pq->gpiqj', w.reshape(h // per, per, n, n), eye).reshape(h // per, tile, tile)


def _layer(xf, memf, batch, depth, w_in, w_pool, pool_scale, conv_w, conv_b, w_rg, b_rg, w_ig, b_ig, lru_L,
           w_mem_kv, w_branch, w_out, ln_g, ln_b, *, tq=256, group=8, t=256, tm=1024, tn=1024):
    dm = xf.shape[1]
    s = xf.shape[0] // batch
    cols = {}
    off = 0
    for name, size in (("q_sb", WIDTH), ("k_sb", WIDTH), ("v_sb", WIDTH), ("g_sb", WIDTH), ("u_pool", WIDTH),
                       ("g_pool", WIDTH), ("u_lru", WIDTH), ("g_lru", WIDTH), ("q_xa", WIDTH), ("g_xa", WIDTH),
                       ("merge", N_BRANCH * dm)):
        cols[name] = w_in[:, off:off + size]
        off += size
    qscale = LOG2E / math.sqrt(HEAD_DIM)
    w_a = jnp.concatenate([cols[n] for n in ("q_sb", "k_sb", "v_sb", "q_xa")], axis=1).astype(BF16)
    scale_a = jnp.concatenate([jnp.full((1, WIDTH), qscale, F32), jnp.ones((1, 2 * WIDTH), F32),
                               jnp.full((1, WIDTH), qscale, F32)], axis=1)
    w_b = jnp.concatenate([cols[n] for n in ("merge", "g_sb", "g_pool", "g_lru", "g_xa", "u_pool", "u_lru")],
                          axis=1).astype(BF16)
    qkv = _matmul(xf, w_a, scale_a, BF16, tm, tn)
    pf = _matmul(xf, w_b, jnp.ones((1, w_b.shape[1]), F32), F32, tm, tn)
    kv = _matmul(memf, w_mem_kv.astype(BF16), jnp.ones((1, 2 * WIDTH), F32), BF16, tm, tn)
    ysb = _sb_attention(qkv.reshape(batch, s, 4 * WIDTH), tq, group).reshape(batch * s, WIDTH)
    row = lambda v: v.reshape(1, -1).astype(F32)
    wts = (w_pool.astype(BF16), row(pool_scale), conv_w.astype(F32), row(conv_b),
           _block_diag_tiles(w_rg, 256).astype(BF16), row(b_rg), _block_diag_tiles(w_ig, 256).astype(BF16), row(b_ig),
           row(lru_L), w_branch.astype(BF16), w_out.astype(BF16), row(ln_g), row(ln_b))
    return _mixer(pf, qkv, ysb, xf, kv, wts, batch, t, (2 * depth) ** 0.25)


def kernel(x, mem, w_in, w_pool, pool_scale, conv_w, conv_b, w_rg, b_rg, w_ig, b_ig, lru_L, w_mem_kv, w_branch,
           w_out, ln_g, ln_b):
    b, s, dm = x.shape
    depth = w_in.shape[0]
    xf = x.reshape(b * s, dm)
    memf = mem.reshape(-1, dm)
    for l in range(depth):
        xf = _layer(xf, memf, b, depth, w_in[l], w_pool[l], pool_scale[l], conv_w[l], conv_b[l], w_rg[l], b_rg[l],
                    w_ig[l], b_ig[l], lru_L[l], w_mem_kv[l], w_branch[l], w_out[l], ln_g[l], ln_b[l])
    return xf.reshape(b, s, dm)
```

```python
import functools
import math

import jax
import jax.numpy as jnp
from jax import lax
from jax.experimental import pallas as pl
from jax.experimental.pallas import tpu as pltpu

F32 = jnp.float32
BF16 = jnp.bfloat16

HEAD_DIM = 128
N_HEADS = 4
WIDTH = N_HEADS * HEAD_DIM
N_BRANCH = 4
POOL_WINDOWS = (2, 4, 8, 16)
LRU_BLOCK_DIM = 64
LRU_C = 8.0
CONV_WIDTH = 4
LN_EPS = 1e-5
SUBLANES = 8
HALO = 16
LOG2E = 1.4426950408889634
SOFTPLUS_CLAMP = 64.0
STICK_EXHAUSTED_BITS = 192.0
VMEM_LIMIT = 56 * 1024 * 1024


def _cparams(sem):
    return pltpu.CompilerParams(dimension_semantics=sem, vmem_limit_bytes=VMEM_LIMIT)


def _mm_kernel(x_ref, w_ref, s_ref, o_ref):
    acc = jnp.dot(x_ref[...].astype(BF16), w_ref[...], preferred_element_type=F32)
    o_ref[...] = (acc * s_ref[...]).astype(o_ref.dtype)


def _matmul(x, w, scale, out_dtype, tm, tn):
    m, k = x.shape
    n = w.shape[1]
    tm, tn = min(tm, m), min(tn, n)
    return pl.pallas_call(
        _mm_kernel,
        grid=(m // tm, n // tn),
        in_specs=[pl.BlockSpec((tm, k), lambda i, j: (i, 0)),
                  pl.BlockSpec((k, tn), lambda i, j: (0, j)),
                  pl.BlockSpec((1, tn), lambda i, j: (0, j))],
        out_specs=pl.BlockSpec((tm, tn), lambda i, j: (i, j)),
        out_shape=jax.ShapeDtypeStruct((m, n), out_dtype),
        compiler_params=_cparams(("parallel", "parallel")),
        name="proj_matmul",
    )(x, w, scale)


def _sb_kernel(q_ref, k_ref, v_ref, t_ref, o_ref, later_ref, d_ref, tot_ref, *, tq, group, sub):
    step = pl.program_id(2)
    tmat = t_ref[...]
    row = lax.broadcasted_iota(jnp.int32, (tq, tq), 0)
    col = lax.broadcasted_iota(jnp.int32, (tq, tq), 1)
    valid = col < row
    reps = tq // HEAD_DIM

    def rows(j):
        return pl.ds(pl.multiple_of(j * tq, tq), tq)

    def scores(q, j):
        return lax.dot_general(q, k_ref[rows(j), :], (((1,), (1,)), ((), ())), preferred_element_type=F32)

    def front(z, mask):
        sp = jnp.maximum(z, jnp.log2(1.0 + jnp.exp2(jnp.minimum(z, SOFTPLUS_CLAMP))))
        if mask is not None:
            sp = jnp.where(mask, sp, 0.0)
        incl = jnp.dot(sp.astype(BF16), tmat, preferred_element_type=F32)
        return z - incl, jnp.broadcast_to(incl[:, 0:1], (tq, HEAD_DIM))

    def back(d, later, v, mask):
        w = jnp.exp2(d + jnp.concatenate([later] * reps, axis=1))
        if mask is not None:
            w = jnp.where(mask, w, 0.0)
        return jnp.dot(w.astype(BF16), v, preferred_element_type=F32)

    def tile_of(i, m):
        return jnp.maximum(i - 1 - m, 0)

    def masked_v(i, m):
        v = v_ref[rows(tile_of(i, m)), :]
        return jnp.where(m < i, v, jnp.zeros_like(v))

    for s in range(sub):
        i = step * sub + s
        own = slice(s * tq, (s + 1) * tq)
        q = q_ref[own, :]
        d0, tot0 = front(scores(q, i), valid)
        later0 = -tot0
        d1, tot1 = front(scores(q, tile_of(i, 0)), None)
        o_ref[own, :] = (back(d0, jnp.zeros((tq, HEAD_DIM), F32), v_ref[rows(i), :], valid)
                         + back(d1, later0, masked_v(i, 0), None))
        later_ref[own, :] = later0 - tot1

    def remaining_visits(i, own):
        q = q_ref[own, :]

        def stick_left():
            return (jnp.max(later_ref[own, :]) > -STICK_EXHAUSTED_BITS).astype(jnp.int32)

        def fronts_to_scratch(g):
            for t in range(group):
                d_ref[t], tot_ref[t] = front(scores(q, tile_of(i, 1 + g * group + t)), None)

        def backs_from_scratch(g):
            later = later_ref[own, :]
            total = None
            for t in range(group):
                part = back(d_ref[t], later, masked_v(i, 1 + g * group + t), None)
                total = part if total is None else total + part
                later = later - tot_ref[t]
            o_ref[own, :] += total
            later_ref[own, :] = later

        n_groups = (jnp.maximum(i - 1, 0) + group - 1) // group
        started = jnp.where(n_groups > 0, stick_left(), 0)

        @pl.when(started == 1)
        def _():
            fronts_to_scratch(0)

        def trip(state):
            g, _ = state
            backs_from_scratch(g)
            fronts_to_scratch(g + 1)
            return g + 1, stick_left()

        last, live = lax.while_loop(lambda state: (state[1] == 1) & (state[0] + 1 < n_groups), trip,
                                    (jnp.int32(0), started))

        @pl.when(live == 1)
        def _():
            backs_from_scratch(last)

    def per_tile(s, carry):
        remaining_visits(step * sub + s, rows(s))
        return carry

    @pl.when(jnp.max(later_ref[...]) > -STICK_EXHAUSTED_BITS)
    def _():
        lax.fori_loop(0, sub, per_tile, 0)


def _sb_attention(qkv, tq, group, sub):
    b, s, _ = qkv.shape
    tq = min(tq, s)
    sub = min(sub, s // tq)
    assert s % (sub * tq) == 0 and tq % HEAD_DIM == 0
    loc = jnp.arange(tq)
    tmat = (loc[:, None] >= loc[None, :]).astype(BF16)
    return pl.pallas_call(
        functools.partial(_sb_kernel, tq=tq, group=group, sub=sub),
        grid=(b, N_HEADS, s // (sub * tq)),
        in_specs=[pl.BlockSpec((None, sub * tq, HEAD_DIM), lambda bi, h, i: (bi, i, h)),
                  pl.BlockSpec((None, s, HEAD_DIM), lambda bi, h, i: (bi, 0, N_HEADS + h)),
                  pl.BlockSpec((None, s, HEAD_DIM), lambda bi, h, i: (bi, 0, 2 * N_HEADS + h)),
                  pl.BlockSpec((tq, tq), lambda bi, h, i: (0, 0))],
        out_specs=pl.BlockSpec((None, sub * tq, HEAD_DIM), lambda bi, h, i: (bi, i, h)),
        out_shape=jax.ShapeDtypeStruct((b, s, WIDTH), F32),
        scratch_shapes=[pltpu.VMEM((sub * tq, HEAD_DIM), F32),
                        pltpu.VMEM((group, tq, tq), F32), pltpu.VMEM((group, tq, HEAD_DIM), F32)],
        compiler_params=_cparams(("parallel", "parallel", "arbitrary")),
        name="stickbreak_attention",
    )(qkv, qkv, qkv, tmat)


def _sigmoid(x):
    return 0.5 + 0.5 * jnp.tanh(0.5 * x)


def _silu(x):
    half = 0.5 * x
    return half + half * jnp.tanh(half)


def _mixer_kernel(x_ref, qx_ref, ysb_ref, kv_ref, wm_ref, wg_ref, wu_ref,
                  wpool_ref, pscale_ref, convw_ref, convb_ref, wrg_ref, brg_ref, wig_ref, big_ref,
                  lrul_ref, wb_ref, wout_ref, lng_ref, lnb_ref, o_ref, h_ref, uprev_ref, *, t, alpha):
    i = pl.program_id(1)

    @pl.when(i == 0)
    def _():
        h_ref[...] = jnp.zeros_like(h_ref)

    x = x_ref[...]
    xb = x.astype(BF16)
    u = jnp.dot(xb, wu_ref[...], preferred_element_type=F32)
    gates = jnp.dot(xb, wg_ref[...], preferred_element_type=F32)
    halo = jnp.where(i == 0, 0.0, uprev_ref[...])
    uprev_ref[...] = u[t - HALO:, :]
    ext = jnp.concatenate([halo, u], axis=0)

    pos = i * t + lax.broadcasted_iota(jnp.int32, (t, HEAD_DIM), 0)
    pool_parts = []
    for g, win in enumerate(POOL_WINDOWS):
        sl = slice(g * HEAD_DIM, (g + 1) * HEAD_DIM)
        s = ext[:, sl]
        d = 1
        while d < win:
            s = s + pltpu.roll(s, d, 0)
            d *= 2
        count = jnp.minimum(pos + 1, win).astype(F32)
        pooled = s[HALO:, :] / count - u[:, sl]
        pool_parts.append(jnp.dot(pooled.astype(BF16), wpool_ref[g], preferred_element_type=F32))
    y_pool = jnp.concatenate(pool_parts, axis=1) * pscale_ref[...] * _silu(gates[:, WIDTH:2 * WIDTH])

    ul = ext[:, WIDTH:]
    cw = convw_ref[...]
    xc = cw[CONV_WIDTH - 1:CONV_WIDTH, :] * ul
    for j in range(1, CONV_WIDTH):
        xc = xc + cw[CONV_WIDTH - 1 - j:CONV_WIDTH - j, :] * pltpu.roll(ul, j, 0)
    xc = xc[HALO:, :] + convb_ref[...]
    xcb = xc.astype(BF16)
    bd = wrg_ref.shape[-1]
    r_pre = jnp.concatenate([jnp.dot(xcb[:, n * bd:(n + 1) * bd], wrg_ref[n], preferred_element_type=F32)
                             for n in range(WIDTH // bd)], axis=1)
    i_pre = jnp.concatenate([jnp.dot(xcb[:, n * bd:(n + 1) * bd], wig_ref[n], preferred_element_type=F32)
                             for n in range(WIDTH // bd)], axis=1)
    r = _sigmoid(r_pre + brg_ref[...])
    ig = _sigmoid(i_pre + big_ref[...])
    neg_l = -lrul_ref[...]
    softplus_neg_l = jnp.maximum(neg_l, 0.0) + jnp.log(1.0 + jnp.exp(-jnp.abs(neg_l)))
    log_a = (-LRU_C * r) * softplus_neg_l
    a = jnp.exp(log_a)
    th = jnp.tanh(log_a)
    bx = jnp.sqrt(-2.0 * th / (1.0 - th)) * (ig * xc)
    in_group = lax.broadcasted_iota(jnp.int32, (t, WIDTH), 0) % SUBLANES
    d = 1
    while d < SUBLANES:
        keep = in_group >= d
        bx = jnp.where(keep, a * pltpu.roll(bx, d, 0) + bx, bx)
        a = jnp.where(keep, a * pltpu.roll(a, d, 0), a)
        d *= 2
    carry = h_ref[...]
    pieces = []
    for g in range(t // SUBLANES):
        sl = slice(g * SUBLANES, (g + 1) * SUBLANES)
        piece = bx[sl, :] + a[sl, :] * carry
        pieces.append(piece)
        carry = piece[SUBLANES - 1:SUBLANES, :]
    h = jnp.concatenate(pieces, axis=0)
    h_ref[...] = carry
    y_lru = h * _silu(gates[:, 2 * WIDTH:3 * WIDTH])

    kv = kv_ref[...]
    qx = qx_ref[...]
    xa_parts = []
    for hh in range(N_HEADS):
        sl = slice(hh * HEAD_DIM, (hh + 1) * HEAD_DIM)
        sc = lax.dot_general(qx[:, sl], kv[:, sl], (((1,), (1,)), ((), ())), preferred_element_type=F32)
        p = jnp.exp2(sc - jnp.max(sc, axis=-1, keepdims=True))
        denom = jnp.sum(p, axis=-1, keepdims=True)
        vh = kv[:, WIDTH + hh * HEAD_DIM:WIDTH + (hh + 1) * HEAD_DIM]
        xa_parts.append(jnp.dot(p.astype(BF16), vh, preferred_element_type=F32) / denom)
    y_xa = jnp.concatenate(xa_parts, axis=1) * _silu(gates[:, 3 * WIDTH:])

    y_sb = ysb_ref[...] * _silu(gates[:, :WIDTH])

    dm = o_ref.shape[-1]
    merged = None
    for n, y in enumerate((y_sb, y_pool, y_lru, y_xa)):
        gate_pre = jnp.dot(xb, wm_ref[:, n * dm:(n + 1) * dm], preferred_element_type=F32)
        term = _sigmoid(gate_pre) * jnp.dot(y.astype(BF16), wb_ref[n], preferred_element_type=F32)
        merged = term if merged is None else merged + term
    out = jnp.dot(merged.astype(BF16), wout_ref[...], preferred_element_type=F32)
    resid = alpha * x + out
    mean = jnp.mean(resid, axis=-1, keepdims=True)
    cen = resid - mean
    var = jnp.mean(cen * cen, axis=-1, keepdims=True)
    o_ref[...] = cen * lax.rsqrt(var + LN_EPS) * lng_ref[...] + lnb_ref[...]


def _mixer(xf, qkv, ysb, kv, wts, batch, t, alpha):
    m, dm = xf.shape
    s = m // batch
    t = min(t, s)
    nt = s // t
    mem_len = kv.shape[0] // batch
    assert s % t == 0 and t % HALO == 0
    row = lambda b, i: b * nt + i

    def resident(arr):
        return pl.BlockSpec(arr.shape, lambda b, i: (0,) * arr.ndim, pipeline_mode=pl.Buffered(1))

    in_specs = [
        pl.BlockSpec((t, dm), lambda b, i: (row(b, i), 0)),
        pl.BlockSpec((t, WIDTH), lambda b, i: (row(b, i), 3)),
        pl.BlockSpec((t, WIDTH), lambda b, i: (row(b, i), 0)),
        pl.BlockSpec((mem_len, 2 * WIDTH), lambda b, i: (b, 0)),
    ] + [resident(w) for w in wts]
    return pl.pallas_call(
        functools.partial(_mixer_kernel, t=t, alpha=alpha),
        grid=(batch, nt),
        in_specs=in_specs,
        out_specs=pl.BlockSpec((t, dm), lambda b, i: (row(b, i), 0)),
        out_shape=jax.ShapeDtypeStruct((m, dm), F32),
        scratch_shapes=[pltpu.VMEM((1, WIDTH), F32), pltpu.VMEM((HALO, 2 * WIDTH), F32)],
        compiler_params=_cparams(("arbitrary", "arbitrary")),
        name="mixer",
    )(xf, qkv, ysb, kv, *wts)


def _block_diag_tiles(w, tile):
    h, n, _ = w.shape
    per = tile // n
    eye = jnp.eye(per, dtype=w.dtype)
    return jnp.einsum('gpij,pq->gpiqj', w.reshape(h // per, per, n, n), eye).reshape(h // per, tile, tile)


def _layer(xf, memf, batch, depth, w_in, w_pool, pool_scale, conv_w, conv_b, w_rg, b_rg, w_ig, b_ig, lru_L,
           w_mem_kv, w_branch, w_out, ln_g, ln_b, *, tq=256, group=8, sub=4, t=512, tm=1024, tn=1024):
    dm = xf.shape[1]
    s = xf.shape[0] // batch
    cols = {}
    off = 0
    for name, size in (("q_sb", WIDTH), ("k_sb", WIDTH), ("v_sb", WIDTH), ("g_sb", WIDTH), ("u_pool", WIDTH),
                       ("g_pool", WIDTH), ("u_lru", WIDTH), ("g_lru", WIDTH), ("q_xa", WIDTH), ("g_xa", WIDTH),
                       ("merge", N_BRANCH * dm)):
        cols[name] = w_in[:, off:off + size]
        off += size
    qscale = LOG2E / math.sqrt(HEAD_DIM)
    w_a = jnp.concatenate([cols[n] for n in ("q_sb", "k_sb", "v_sb", "q_xa")], axis=1).astype(BF16)
    scale_a = jnp.concatenate([jnp.full((1, WIDTH), qscale, F32), jnp.ones((1, 2 * WIDTH), F32),
                               jnp.full((1, WIDTH), qscale, F32)], axis=1)
    w_gate = jnp.concatenate([cols[n] for n in ("g_sb", "g_pool", "g_lru", "g_xa")], axis=1).astype(BF16)
    w_u = jnp.concatenate([cols[n] for n in ("u_pool", "u_lru")], axis=1).astype(BF16)
    qkv = _matmul(xf, w_a, scale_a, BF16, tm, tn)
    kv = _matmul(memf, w_mem_kv.astype(BF16), jnp.ones((1, 2 * WIDTH), F32), BF16, tm, tn)
    ysb = _sb_attention(qkv.reshape(batch, s, 4 * WIDTH), tq, group, sub).reshape(batch * s, WIDTH)
    row = lambda v: v.reshape(1, -1).astype(F32)
    wts = (cols["merge"].astype(BF16), w_gate, w_u,
           w_pool.astype(BF16), row(pool_scale), conv_w.astype(F32), row(conv_b),
           _block_diag_tiles(w_rg, 256).astype(BF16), row(b_rg), _block_diag_tiles(w_ig, 256).astype(BF16), row(b_ig),
           row(lru_L), w_branch.astype(BF16), w_out.astype(BF16), row(ln_g), row(ln_b))
    return _mixer(xf, qkv, ysb, kv, wts, batch, t, (2 * depth) ** 0.25)


def kernel(x, mem, w_in, w_pool, pool_scale, conv_w, conv_b, w_rg, b_rg, w_ig, b_ig, lru_L, w_mem_kv, w_branch,
           w_out, ln_g, ln_b):
    b, s, dm = x.shape
    depth = w_in.shape[0]
    xf = x.reshape(b * s, dm)
    memf = mem.reshape(-1, dm)
    for l in range(depth):
        xf = _layer(xf, memf, b, depth, w_in[l], w_pool[l], pool_scale[l], conv_w[l], conv_b[l], w_rg[l], b_rg[l],
                    w_ig[l], b_ig[l], lru_L[l], w_mem_kv[l], w_branch[l], w_out[l], ln_g[l], ln_b[l])
    return xf.reshape(b, s, dm)
```

```python
import functools
import math

import jax
import jax.numpy as jnp
from jax import lax
from jax.experimental import pallas as pl
from jax.experimental.pallas import tpu as pltpu

F32 = jnp.float32
BF16 = jnp.bfloat16

HEAD_DIM = 128
N_HEADS = 4
WIDTH = N_HEADS * HEAD_DIM
N_BRANCH = 4
POOL_WINDOWS = (2, 4, 8, 16)
LRU_BLOCK_DIM = 64
LRU_C = 8.0
CONV_WIDTH = 4
LN_EPS = 1e-5
SUBLANES = 8
HALO = 16
LOG2E = 1.4426950408889634
SOFTPLUS_CLAMP = 64.0
STICK_EXHAUSTED_BITS = 192.0
VMEM_LIMIT = 56 * 1024 * 1024


def _cparams(sem):
    return pltpu.CompilerParams(dimension_semantics=sem, vmem_limit_bytes=VMEM_LIMIT)


def _mm_kernel(x_ref, w_ref, s_ref, o_ref):
    acc = jnp.dot(x_ref[...].astype(BF16), w_ref[...], preferred_element_type=F32)
    o_ref[...] = (acc * s_ref[...]).astype(o_ref.dtype)


def _matmul(x, w, scale, out_dtype, tm, tn):
    m, k = x.shape
    n = w.shape[1]
    tm, tn = min(tm, m), min(tn, n)
    return pl.pallas_call(
        _mm_kernel,
        grid=(m // tm, n // tn),
        in_specs=[pl.BlockSpec((tm, k), lambda i, j: (i, 0)),
                  pl.BlockSpec((k, tn), lambda i, j: (0, j)),
                  pl.BlockSpec((1, tn), lambda i, j: (0, j))],
        out_specs=pl.BlockSpec((tm, tn), lambda i, j: (i, j)),
        out_shape=jax.ShapeDtypeStruct((m, n), out_dtype),
        compiler_params=_cparams(("parallel", "parallel")),
        name="proj_matmul",
    )(x, w, scale)


def _sb_kernel(q_ref, k_ref, v_ref, t_ref, o_ref, later_ref, d_ref, tot_ref, *, tq, group, sub):
    step = pl.program_id(2)
    tmat = t_ref[...]
    row = lax.broadcasted_iota(jnp.int32, (tq, tq), 0)
    col = lax.broadcasted_iota(jnp.int32, (tq, tq), 1)
    valid = col < row
    reps = tq // HEAD_DIM

    def rows(j):
        return pl.ds(pl.multiple_of(j * tq, tq), tq)

    def scores(q, j):
        return lax.dot_general(q, k_ref[rows(j), :], (((1,), (1,)), ((), ())), preferred_element_type=F32)

    def front(z, mask):
        sp = jnp.maximum(z, jnp.log2(1.0 + jnp.exp2(jnp.minimum(z, SOFTPLUS_CLAMP))))
        if mask is not None:
            sp = jnp.where(mask, sp, 0.0)
        incl = jnp.dot(sp.astype(BF16), tmat, preferred_element_type=F32)
        return z - incl, jnp.broadcast_to(incl[:, 0:1], (tq, HEAD_DIM))

    def back(d, later, v, mask):
        w = jnp.exp2(d + jnp.concatenate([later] * reps, axis=1))
        if mask is not None:
            w = jnp.where(mask, w, 0.0)
        return jnp.dot(w.astype(BF16), v, preferred_element_type=F32)

    def tile_of(i, m):
        return jnp.maximum(i - 1 - m, 0)

    def masked_v(i, m):
        v = v_ref[rows(tile_of(i, m)), :]
        return jnp.where(m < i, v, jnp.zeros_like(v))

    tiles = []
    for s in range(sub):
        i = step * sub + s
        own = slice(s * tq, (s + 1) * tq)
        q = q_ref[own, :]
        tiles.append((i, own, scores(q, i), scores(q, tile_of(i, 0))))
    fronts = [(front(z_diag, valid), front(z_below, None)) for _, _, z_diag, z_below in tiles]
    for (i, own, _, _), ((d0, tot0), (d1, tot1)) in zip(tiles, fronts):
        later0 = -tot0
        o_ref[own, :] = (back(d0, jnp.zeros((tq, HEAD_DIM), F32), v_ref[rows(i), :], valid)
                         + back(d1, later0, masked_v(i, 0), None))
        later_ref[own, :] = later0 - tot1

    def remaining_visits(i, own):
        q = q_ref[own, :]

        def stick_left():
            return (jnp.max(later_ref[own, :]) > -STICK_EXHAUSTED_BITS).astype(jnp.int32)

        def fronts_to_scratch(g):
            for t in range(group):
                d_ref[t], tot_ref[t] = front(scores(q, tile_of(i, 1 + g * group + t)), None)

        def backs_from_scratch(g):
            later = later_ref[own, :]
            total = None
            for t in range(group):
                part = back(d_ref[t], later, masked_v(i, 1 + g * group + t), None)
                total = part if total is None else total + part
                later = later - tot_ref[t]
            o_ref[own, :] += total
            later_ref[own, :] = later

        n_groups = (jnp.maximum(i - 1, 0) + group - 1) // group
        started = jnp.where(n_groups > 0, stick_left(), 0)

        @pl.when(started == 1)
        def _():
            fronts_to_scratch(0)

        def trip(state):
            g, _ = state
            backs_from_scratch(g)
            fronts_to_scratch(g + 1)
            return g + 1, stick_left()

        last, live = lax.while_loop(lambda state: (state[1] == 1) & (state[0] + 1 < n_groups), trip,
                                    (jnp.int32(0), started))

        @pl.when(live == 1)
        def _():
            backs_from_scratch(last)

    def per_tile(s, carry):
        remaining_visits(step * sub + s, rows(s))
        return carry

    @pl.when(jnp.max(later_ref[...]) > -STICK_EXHAUSTED_BITS)
    def _():
        lax.fori_loop(0, sub, per_tile, 0)


def _sb_attention(qkv, tq, group, sub):
    b, s, _ = qkv.shape
    tq = min(tq, s)
    sub = min(sub, s // tq)
    assert s % (sub * tq) == 0 and tq % HEAD_DIM == 0
    loc = jnp.arange(tq)
    tmat = (loc[:, None] >= loc[None, :]).astype(BF16)
    return pl.pallas_call(
        functools.partial(_sb_kernel, tq=tq, group=group, sub=sub),
        grid=(b, N_HEADS, s // (sub * tq)),
        in_specs=[pl.BlockSpec((None, sub * tq, HEAD_DIM), lambda bi, h, i: (bi, i, h)),
                  pl.BlockSpec((None, s, HEAD_DIM), lambda bi, h, i: (bi, 0, N_HEADS + h)),
                  pl.BlockSpec((None, s, HEAD_DIM), lambda bi, h, i: (bi, 0, 2 * N_HEADS + h)),
                  pl.BlockSpec((tq, tq), lambda bi, h, i: (0, 0))],
        out_specs=pl.BlockSpec((None, sub * tq, HEAD_DIM), lambda bi, h, i: (bi, i, h)),
        out_shape=jax.ShapeDtypeStruct((b, s, WIDTH), F32),
        scratch_shapes=[pltpu.VMEM((sub * tq, HEAD_DIM), F32),
                        pltpu.VMEM((group, tq, tq), F32), pltpu.VMEM((group, tq, HEAD_DIM), F32)],
        compiler_params=_cparams(("parallel", "parallel", "arbitrary")),
        name="stickbreak_attention",
    )(qkv, qkv, qkv, tmat)


def _sigmoid(x):
    return 0.5 + 0.5 * jnp.tanh(0.5 * x)


def _silu(x):
    half = 0.5 * x
    return half + half * jnp.tanh(half)


def _mixer_kernel(x_ref, qx_ref, ysb_ref, kv_ref, wm_ref, wg_ref, wu_ref,
                  wpool_ref, pscale_ref, convw_ref, convb_ref, wrg_ref, brg_ref, wig_ref, big_ref,
                  lrul_ref, wb_ref, wout_ref, lng_ref, lnb_ref, o_ref, h_ref, uprev_ref, *, t, alpha):
    i = pl.program_id(1)

    @pl.when(i == 0)
    def _():
        h_ref[...] = jnp.zeros_like(h_ref)

    x = x_ref[...]
    xb = x.astype(BF16)
    u = jnp.dot(xb, wu_ref[...], preferred_element_type=F32)
    gates = jnp.dot(xb, wg_ref[...], preferred_element_type=F32)
    halo = jnp.where(i == 0, 0.0, uprev_ref[...])
    uprev_ref[...] = u[t - HALO:, :]
    ext = jnp.concatenate([halo, u], axis=0)

    pos = i * t + lax.broadcasted_iota(jnp.int32, (t, HEAD_DIM), 0)
    pool_parts = []
    for g, win in enumerate(POOL_WINDOWS):
        sl = slice(g * HEAD_DIM, (g + 1) * HEAD_DIM)
        s = ext[:, sl]
        d = 1
        while d < win:
            s = s + pltpu.roll(s, d, 0)
            d *= 2
        count = jnp.minimum(pos + 1, win).astype(F32)
        pooled = s[HALO:, :] / count - u[:, sl]
        pool_parts.append(jnp.dot(pooled.astype(BF16), wpool_ref[g], preferred_element_type=F32))
    y_pool = jnp.concatenate(pool_parts, axis=1) * pscale_ref[...] * _silu(gates[:, WIDTH:2 * WIDTH])

    ul = ext[:, WIDTH:]
    cw = convw_ref[...]
    xc = cw[CONV_WIDTH - 1:CONV_WIDTH, :] * ul
    for j in range(1, CONV_WIDTH):
        xc = xc + cw[CONV_WIDTH - 1 - j:CONV_WIDTH - j, :] * pltpu.roll(ul, j, 0)
    xc = xc[HALO:, :] + convb_ref[...]
    xcb = xc.astype(BF16)
    bd = wrg_ref.shape[-1]
    r_pre = jnp.concatenate([jnp.dot(xcb[:, n * bd:(n + 1) * bd], wrg_ref[n], preferred_element_type=F32)
                             for n in range(WIDTH // bd)], axis=1)
    i_pre = jnp.concatenate([jnp.dot(xcb[:, n * bd:(n + 1) * bd], wig_ref[n], preferred_element_type=F32)
                             for n in range(WIDTH // bd)], axis=1)
    r = _sigmoid(r_pre + brg_ref[...])
    ig = _sigmoid(i_pre + big_ref[...])
    neg_l = -lrul_ref[...]
    softplus_neg_l = jnp.maximum(neg_l, 0.0) + jnp.log(1.0 + jnp.exp(-jnp.abs(neg_l)))
    log_a = (-LRU_C * r) * softplus_neg_l
    a = jnp.exp(log_a)
    th = jnp.tanh(log_a)
    bx = jnp.sqrt(-2.0 * th / (1.0 - th)) * (ig * xc)
    in_group = lax.broadcasted_iota(jnp.int32, (t, WIDTH), 0) % SUBLANES
    d = 1
    while d < SUBLANES:
        keep = in_group >= d
        bx = jnp.where(keep, a * pltpu.roll(bx, d, 0) + bx, bx)
        a = jnp.where(keep, a * pltpu.roll(a, d, 0), a)
        d *= 2
    carry = h_ref[...]
    pieces = []
    for g in range(t // SUBLANES):
        sl = slice(g * SUBLANES, (g + 1) * SUBLANES)
        piece = bx[sl, :] + a[sl, :] * carry
        pieces.append(piece)
        carry = piece[SUBLANES - 1:SUBLANES, :]
    h = jnp.concatenate(pieces, axis=0)
    h_ref[...] = carry
    y_lru = h * _silu(gates[:, 2 * WIDTH:3 * WIDTH])

    kv = kv_ref[...]
    qx = qx_ref[...]
    xa_parts = []
    for hh in range(N_HEADS):
        sl = slice(hh * HEAD_DIM, (hh + 1) * HEAD_DIM)
        sc = lax.dot_general(qx[:, sl], kv[:, sl], (((1,), (1,)), ((), ())), preferred_element_type=F32)
        p = jnp.exp2(sc - jnp.max(sc, axis=-1, keepdims=True))
        denom = jnp.sum(p, axis=-1, keepdims=True)
        vh = kv[:, WIDTH + hh * HEAD_DIM:WIDTH + (hh + 1) * HEAD_DIM]
        xa_parts.append(jnp.dot(p.astype(BF16), vh, preferred_element_type=F32) / denom)
    y_xa = jnp.concatenate(xa_parts, axis=1) * _silu(gates[:, 3 * WIDTH:])

    y_sb = ysb_ref[...] * _silu(gates[:, :WIDTH])

    dm = o_ref.shape[-1]
    merged = None
    for n, y in enumerate((y_sb, y_pool, y_lru, y_xa)):
        gate_pre = jnp.dot(xb, wm_ref[:, n * dm:(n + 1) * dm], preferred_element_type=F32)
        term = _sigmoid(gate_pre) * jnp.dot(y.astype(BF16), wb_ref[n], preferred_element_type=F32)
        merged = term if merged is None else merged + term
    out = jnp.dot(merged.astype(BF16), wout_ref[...], preferred_element_type=F32)
    resid = alpha * x + out
    mean = jnp.mean(resid, axis=-1, keepdims=True)
    cen = resid - mean
    var = jnp.mean(cen * cen, axis=-1, keepdims=True)
    o_ref[...] = cen * lax.rsqrt(var + LN_EPS) * lng_ref[...] + lnb_ref[...]


def _mixer(xf, qkv, ysb, kv, wts, batch, t, alpha):
    m, dm = xf.shape
    s = m // batch
    t = min(t, s)
    nt = s // t
    mem_len = kv.shape[0] // batch
    assert s % t == 0 and t % HALO == 0
    row = lambda b, i: b * nt + i

    def resident(arr):
        return pl.BlockSpec(arr.shape, lambda b, i: (0,) * arr.ndim, pipeline_mode=pl.Buffered(1))

    in_specs = [
        pl.BlockSpec((t, dm), lambda b, i: (row(b, i), 0)),
        pl.BlockSpec((t, WIDTH), lambda b, i: (row(b, i), 3)),
        pl.BlockSpec((t, WIDTH), lambda b, i: (row(b, i), 0)),
        pl.BlockSpec((mem_len, 2 * WIDTH), lambda b, i: (b, 0)),
    ] + [resident(w) for w in wts]
    return pl.pallas_call(
        functools.partial(_mixer_kernel, t=t, alpha=alpha),
        grid=(batch, nt),
        in_specs=in_specs,
        out_specs=pl.BlockSpec((t, dm), lambda b, i: (row(b, i), 0)),
        out_shape=jax.ShapeDtypeStruct((m, dm), F32),
        scratch_shapes=[pltpu.VMEM((1, WIDTH), F32), pltpu.VMEM((HALO, 2 * WIDTH), F32)],
        compiler_params=_cparams(("arbitrary", "arbitrary")),
        name="mixer",
    )(xf, qkv, ysb, kv, *wts)


def _block_diag_tiles(w, tile):
    h, n, _ = w.shape
    per = tile // n
    eye = jnp.eye(per, dtype=w.dtype)
    return jnp.einsum('gpij,pq->gpiqj', w.reshape(h // per, per, n, n), eye).reshape(h // per, tile, tile)


def _layer(xf, memf, batch, depth, w_in, w_pool, pool_scale, conv_w, conv_b, w_rg, b_rg, w_ig, b_ig, lru_L,
           w_mem_kv, w_branch, w_out, ln_g, ln_b, *, tq=256, group=8, sub=8, t=512, tm=1024, tn=1024):
    dm = xf.shape[1]
    s = xf.shape[0] // batch
    cols = {}
    off = 0
    for name, size in (("q_sb", WIDTH), ("k_sb", WIDTH), ("v_sb", WIDTH), ("g_sb", WIDTH), ("u_pool", WIDTH),
                       ("g_pool", WIDTH), ("u_lru", WIDTH), ("g_lru", WIDTH), ("q_xa", WIDTH), ("g_xa", WIDTH),
                       ("merge", N_BRANCH * dm)):
        cols[name] = w_in[:, off:off + size]
        off += size
    qscale = LOG2E / math.sqrt(HEAD_DIM)
    w_a = jnp.concatenate([cols[n] for n in ("q_sb", "k_sb", "v_sb", "q_xa")], axis=1).astype(BF16)
    scale_a = jnp.concatenate([jnp.full((1, WIDTH), qscale, F32), jnp.ones((1, 2 * WIDTH), F32),
                               jnp.full((1, WIDTH), qscale, F32)], axis=1)
    w_gate = jnp.concatenate([cols[n] for n in ("g_sb", "g_pool", "g_lru", "g_xa")], axis=1).astype(BF16)
    w_u = jnp.concatenate([cols[n] for n in ("u_pool", "u_lru")], axis=1).astype(BF16)
    qkv = _matmul(xf, w_a, scale_a, BF16, tm, tn)
    kv = _matmul(memf, w_mem_kv.astype(BF16), jnp.ones((1, 2 * WIDTH), F32), BF16, tm, tn)
    ysb = _sb_attention(qkv.reshape(batch, s, 4 * WIDTH), tq, group, sub).reshape(batch * s, WIDTH)
    row = lambda v: v.reshape(1, -1).astype(F32)
    wts = (cols["merge"].astype(BF16), w_gate, w_u,
           w_pool.astype(BF16), row(pool_scale), conv_w.astype(F32), row(conv_b),
           _block_diag_tiles(w_rg, 256).astype(BF16), row(b_rg), _block_diag_tiles(w_ig, 256).astype(BF16), row(b_ig),
           row(lru_L), w_branch.astype(BF16), w_out.astype(BF16), row(ln_g), row(ln_b))
    return _mixer(xf, qkv, ysb, kv, wts, batch, t, (2 * depth) ** 0.25)


def kernel(x, mem, w_in, w_pool, pool_scale, conv_w, conv_b, w_rg, b_rg, w_ig, b_ig, lru_L, w_mem_kv, w_branch,
           w_out, ln_g, ln_b):
    b, s, dm = x.shape
    depth = w_in.shape[0]
    xf = x.reshape(b * s, dm)
    memf = mem.reshape(-1, dm)
    for l in range(depth):
        xf = _layer(xf, memf, b, depth, w_in[l], w_pool[l], pool_scale[l], conv_w[l], conv_b[l], w_rg[l], b_rg[l],
                    w_ig[l], b_ig[l], lru_L[l], w_mem_kv[l], w_branch[l], w_out[l], ln_g[l], ln_b[l])
    return xf.reshape(b, s, dm)
```

```python
import functools
import math

import jax
import jax.numpy as jnp
from jax import lax
from jax.experimental import pallas as pl
from jax.experimental.pallas import tpu as pltpu

F32 = jnp.float32
BF16 = jnp.bfloat16

HEAD_DIM = 128
N_HEADS = 4
WIDTH = N_HEADS * HEAD_DIM
N_BRANCH = 4
POOL_WINDOWS = (2, 4, 8, 16)
LRU_BLOCK_DIM = 64
LRU_C = 8.0
CONV_WIDTH = 4
LN_EPS = 1e-5
SUBLANES = 8
MXU_TILE = 256
QKV_XA_BLOCK = 3
HALO = 16
LOG2E = 1.4426950408889634
SOFTPLUS_CLAMP = 64.0
STICK_EXHAUSTED_BITS = 192.0
VMEM_LIMIT = 56 * 1024 * 1024


def _cparams(sem):
    return pltpu.CompilerParams(dimension_semantics=sem, vmem_limit_bytes=VMEM_LIMIT)


def _mm_kernel(x_ref, w_ref, s_ref, o_ref):
    acc = jnp.dot(x_ref[...].astype(BF16), w_ref[...], preferred_element_type=F32)
    o_ref[...] = (acc * s_ref[...]).astype(o_ref.dtype)


def _matmul(x, w, scale, out_dtype, tm, tn):
    m, k = x.shape
    n = w.shape[1]
    tm, tn = min(tm, m), min(tn, n)
    return pl.pallas_call(
        _mm_kernel,
        grid=(m // tm, n // tn),
        in_specs=[pl.BlockSpec((tm, k), lambda i, j: (i, 0)),
                  pl.BlockSpec((k, tn), lambda i, j: (0, j)),
                  pl.BlockSpec((1, tn), lambda i, j: (0, j))],
        out_specs=pl.BlockSpec((tm, tn), lambda i, j: (i, j)),
        out_shape=jax.ShapeDtypeStruct((m, n), out_dtype),
        compiler_params=_cparams(("parallel", "parallel")),
        name="proj_matmul",
    )(x, w, scale)


def _sb_kernel(q_ref, k_ref, v_ref, t_ref, o_ref, later_ref, d_ref, tot_ref, *, tq, group, sub):
    step = pl.program_id(2)
    tmat = t_ref[...]
    row = lax.broadcasted_iota(jnp.int32, (tq, tq), 0)
    col = lax.broadcasted_iota(jnp.int32, (tq, tq), 1)
    valid = col < row
    reps = tq // HEAD_DIM

    def rows(j):
        return pl.ds(pl.multiple_of(j * tq, tq), tq)

    def scores(q, j):
        return lax.dot_general(q, k_ref[rows(j), :], (((1,), (1,)), ((), ())), preferred_element_type=F32)

    def front(z, mask):
        sp = jnp.maximum(z, jnp.log2(1.0 + jnp.exp2(jnp.minimum(z, SOFTPLUS_CLAMP))))
        if mask is not None:
            sp = jnp.where(mask, sp, 0.0)
        incl = jnp.dot(sp.astype(BF16), tmat, preferred_element_type=F32)
        return z - incl, jnp.broadcast_to(incl[:, 0:1], (tq, HEAD_DIM))

    def back(d, later, v, mask):
        w = jnp.exp2(d + jnp.concatenate([later] * reps, axis=1))
        if mask is not None:
            w = jnp.where(mask, w, 0.0)
        return jnp.dot(w.astype(BF16), v, preferred_element_type=F32)

    def tile_of(i, m):
        return jnp.maximum(i - 1 - m, 0)

    def masked_v(i, m):
        v = v_ref[rows(tile_of(i, m)), :]
        return jnp.where(m < i, v, jnp.zeros_like(v))

    tiles = []
    for s in range(sub):
        i = step * sub + s
        own = slice(s * tq, (s + 1) * tq)
        q = q_ref[own, :]
        tiles.append((i, own, scores(q, i), scores(q, tile_of(i, 0))))
    fronts = [(front(z_diag, valid), front(z_below, None)) for _, _, z_diag, z_below in tiles]
    for (i, own, _, _), ((d0, tot0), (d1, tot1)) in zip(tiles, fronts):
        later0 = -tot0
        o_ref[own, :] = (back(d0, jnp.zeros((tq, HEAD_DIM), F32), v_ref[rows(i), :], valid)
                         + back(d1, later0, masked_v(i, 0), None))
        later_ref[own, :] = later0 - tot1

    def remaining_visits(i, own):
        q = q_ref[own, :]

        def stick_left():
            return (jnp.max(later_ref[own, :]) > -STICK_EXHAUSTED_BITS).astype(jnp.int32)

        def fronts_to_scratch(g):
            for t in range(group):
                d_ref[t], tot_ref[t] = front(scores(q, tile_of(i, 1 + g * group + t)), None)

        def backs_from_scratch(g):
            later = later_ref[own, :]
            total = None
            for t in range(group):
                part = back(d_ref[t], later, masked_v(i, 1 + g * group + t), None)
                total = part if total is None else total + part
                later = later - tot_ref[t]
            o_ref[own, :] += total
            later_ref[own, :] = later

        n_groups = (jnp.maximum(i - 1, 0) + group - 1) // group
        started = jnp.where(n_groups > 0, stick_left(), 0)

        @pl.when(started == 1)
        def _():
            fronts_to_scratch(0)

        def trip(state):
            g, _ = state
            backs_from_scratch(g)
            fronts_to_scratch(g + 1)
            return g + 1, stick_left()

        last, live = lax.while_loop(lambda state: (state[1] == 1) & (state[0] + 1 < n_groups), trip,
                                    (jnp.int32(0), started))

        @pl.when(live == 1)
        def _():
            backs_from_scratch(last)

    def per_tile(s, carry):
        remaining_visits(step * sub + s, rows(s))
        return carry

    @pl.when(jnp.max(later_ref[...]) > -STICK_EXHAUSTED_BITS)
    def _():
        lax.fori_loop(0, sub, per_tile, 0)


def _sb_attention(qkv, tq, group, sub):
    b, s, _ = qkv.shape
    tq = min(tq, s)
    sub = min(sub, s // tq)
    assert s % (sub * tq) == 0 and tq % HEAD_DIM == 0
    loc = jnp.arange(tq)
    tmat = (loc[:, None] >= loc[None, :]).astype(BF16)
    return pl.pallas_call(
        functools.partial(_sb_kernel, tq=tq, group=group, sub=sub),
        grid=(b, N_HEADS, s // (sub * tq)),
        in_specs=[pl.BlockSpec((None, sub * tq, HEAD_DIM), lambda bi, h, i: (bi, i, h)),
                  pl.BlockSpec((None, s, HEAD_DIM), lambda bi, h, i: (bi, 0, N_HEADS + h)),
                  pl.BlockSpec((None, s, HEAD_DIM), lambda bi, h, i: (bi, 0, 2 * N_HEADS + h)),
                  pl.BlockSpec((tq, tq), lambda bi, h, i: (0, 0))],
        out_specs=pl.BlockSpec((None, sub * tq, HEAD_DIM), lambda bi, h, i: (bi, i, h)),
        out_shape=jax.ShapeDtypeStruct((b, s, WIDTH), F32),
        scratch_shapes=[pltpu.VMEM((sub * tq, HEAD_DIM), F32),
                        pltpu.VMEM((group, tq, tq), F32), pltpu.VMEM((group, tq, HEAD_DIM), F32)],
        compiler_params=_cparams(("parallel", "parallel", "arbitrary")),
        name="stickbreak_attention",
    )(qkv, qkv, qkv, tmat)


def _sigmoid(x):
    return 0.5 + 0.5 * jnp.tanh(0.5 * x)


def _silu(x):
    half = 0.5 * x
    return half + half * jnp.tanh(half)


def _mixer_kernel(x_ref, qx_ref, ysb_ref, kv_ref, wm_ref, wg_ref, wu_ref,
                  wpool_ref, pscale_ref, convw_ref, convb_ref, wrg_ref, brg_ref, wig_ref, big_ref,
                  lrul_ref, wb_ref, wout_ref, lng_ref, lnb_ref, o_ref, h_ref, uprev_ref, *, t, alpha):
    i = pl.program_id(1)

    @pl.when(i == 0)
    def _():
        h_ref[...] = jnp.zeros_like(h_ref)

    x = x_ref[...]
    xb = x.astype(BF16)
    u = jnp.dot(xb, wu_ref[...], preferred_element_type=F32)
    gates = jnp.dot(xb, wg_ref[...], preferred_element_type=F32)
    halo = jnp.where(i == 0, 0.0, uprev_ref[...])
    uprev_ref[...] = u[t - HALO:, :]
    ext = jnp.concatenate([halo, u], axis=0)

    pos = i * t + lax.broadcasted_iota(jnp.int32, (t, HEAD_DIM), 0)
    pool_parts = []
    for g, win in enumerate(POOL_WINDOWS):
        sl = slice(g * HEAD_DIM, (g + 1) * HEAD_DIM)
        s = ext[:, sl]
        d = 1
        while d < win:
            s = s + pltpu.roll(s, d, 0)
            d *= 2
        count = jnp.minimum(pos + 1, win).astype(F32)
        pooled = s[HALO:, :] / count - u[:, sl]
        pool_parts.append(jnp.dot(pooled.astype(BF16), wpool_ref[g], preferred_element_type=F32))
    y_pool = jnp.concatenate(pool_parts, axis=1) * pscale_ref[...] * _silu(gates[:, WIDTH:2 * WIDTH])

    ul = ext[:, WIDTH:]
    cw = convw_ref[...]
    xc = cw[CONV_WIDTH - 1:CONV_WIDTH, :] * ul
    for j in range(1, CONV_WIDTH):
        xc = xc + cw[CONV_WIDTH - 1 - j:CONV_WIDTH - j, :] * pltpu.roll(ul, j, 0)
    xc = xc[HALO:, :] + convb_ref[...]
    xcb = xc.astype(BF16)
    bd = wrg_ref.shape[-1]
    r_pre = jnp.concatenate([jnp.dot(xcb[:, n * bd:(n + 1) * bd], wrg_ref[n], preferred_element_type=F32)
                             for n in range(WIDTH // bd)], axis=1)
    i_pre = jnp.concatenate([jnp.dot(xcb[:, n * bd:(n + 1) * bd], wig_ref[n], preferred_element_type=F32)
                             for n in range(WIDTH // bd)], axis=1)
    r = _sigmoid(r_pre + brg_ref[...])
    ig = _sigmoid(i_pre + big_ref[...])
    neg_l = -lrul_ref[...]
    softplus_neg_l = jnp.maximum(neg_l, 0.0) + jnp.log(1.0 + jnp.exp(-jnp.abs(neg_l)))
    log_a = (-LRU_C * r) * softplus_neg_l
    a = jnp.exp(log_a)
    th = jnp.tanh(log_a)
    bx = jnp.sqrt(-2.0 * th / (1.0 - th)) * (ig * xc)
    in_group = lax.broadcasted_iota(jnp.int32, (t, WIDTH), 0) % SUBLANES
    d = 1
    while d < SUBLANES:
        keep = in_group >= d
        bx = jnp.where(keep, a * pltpu.roll(bx, d, 0) + bx, bx)
        a = jnp.where(keep, a * pltpu.roll(a, d, 0), a)
        d *= 2
    carry = h_ref[...]
    pieces = []
    for g in range(t // SUBLANES):
        sl = slice(g * SUBLANES, (g + 1) * SUBLANES)
        piece = bx[sl, :] + a[sl, :] * carry
        pieces.append(piece)
        carry = piece[SUBLANES - 1:SUBLANES, :]
    h = jnp.concatenate(pieces, axis=0)
    h_ref[...] = carry
    y_lru = h * _silu(gates[:, 2 * WIDTH:3 * WIDTH])

    kv = kv_ref[...]
    qx = qx_ref[...]
    xa_parts = []
    for hh in range(N_HEADS):
        sl = slice(hh * HEAD_DIM, (hh + 1) * HEAD_DIM)
        sc = lax.dot_general(qx[:, sl], kv[:, sl], (((1,), (1,)), ((), ())), preferred_element_type=F32)
        p = jnp.exp2(sc - jnp.max(sc, axis=-1, keepdims=True))
        denom = jnp.sum(p, axis=-1, keepdims=True)
        vh = kv[:, WIDTH + hh * HEAD_DIM:WIDTH + (hh + 1) * HEAD_DIM]
        xa_parts.append(jnp.dot(p.astype(BF16), vh, preferred_element_type=F32) / denom)
    y_xa = jnp.concatenate(xa_parts, axis=1) * _silu(gates[:, 3 * WIDTH:])

    y_sb = ysb_ref[...] * _silu(gates[:, :WIDTH])

    dm = o_ref.shape[-1]
    merged = None
    for n, y in enumerate((y_sb, y_pool, y_lru, y_xa)):
        gate_pre = jnp.dot(xb, wm_ref[:, n * dm:(n + 1) * dm], preferred_element_type=F32)
        term = _sigmoid(gate_pre) * jnp.dot(y.astype(BF16), wb_ref[n], preferred_element_type=F32)
        merged = term if merged is None else merged + term
    out = jnp.dot(merged.astype(BF16), wout_ref[...], preferred_element_type=F32)
    resid = alpha * x + out
    mean = jnp.mean(resid, axis=-1, keepdims=True)
    cen = resid - mean
    var = jnp.mean(cen * cen, axis=-1, keepdims=True)
    o_ref[...] = cen * lax.rsqrt(var + LN_EPS) * lng_ref[...] + lnb_ref[...]


def _mixer(xf, qkv, ysb, kv, wts, batch, t, alpha):
    m, dm = xf.shape
    s = m // batch
    t = min(t, s)
    nt = s // t
    mem_len = kv.shape[0] // batch
    assert s % t == 0 and t % HALO == 0
    row = lambda b, i: b * nt + i

    def resident(arr):
        return pl.BlockSpec(arr.shape, lambda b, i: (0,) * arr.ndim, pipeline_mode=pl.Buffered(1))

    in_specs = [
        pl.BlockSpec((t, dm), lambda b, i: (row(b, i), 0)),
        pl.BlockSpec((t, WIDTH), lambda b, i: (row(b, i), QKV_XA_BLOCK)),
        pl.BlockSpec((t, WIDTH), lambda b, i: (row(b, i), 0)),
        pl.BlockSpec((mem_len, 2 * WIDTH), lambda b, i: (b, 0)),
    ] + [resident(w) for w in wts]
    return pl.pallas_call(
        functools.partial(_mixer_kernel, t=t, alpha=alpha),
        grid=(batch, nt),
        in_specs=in_specs,
        out_specs=pl.BlockSpec((t, dm), lambda b, i: (row(b, i), 0)),
        out_shape=jax.ShapeDtypeStruct((m, dm), F32),
        scratch_shapes=[pltpu.VMEM((1, WIDTH), F32), pltpu.VMEM((HALO, 2 * WIDTH), F32)],
        compiler_params=_cparams(("arbitrary", "arbitrary")),
        name="mixer",
    )(xf, qkv, ysb, kv, *wts)


def _block_diag_tiles(w, tile):
    h, n, _ = w.shape
    per = tile // n
    eye = jnp.eye(per, dtype=w.dtype)
    return jnp.einsum('gpij,pq->gpiqj', w.reshape(h // per, per, n, n), eye).reshape(h // per, tile, tile)


def _layer(xf, memf, batch, depth, w_in, w_pool, pool_scale, conv_w, conv_b, w_rg, b_rg, w_ig, b_ig, lru_L,
           w_mem_kv, w_branch, w_out, ln_g, ln_b, *, tq=256, group=8, sub=8, t=512, tm=1024, tn=2048):
    dm = xf.shape[1]
    s = xf.shape[0] // batch
    w_in = w_in.astype(BF16)
    cols = {}
    off = 0
    for name, size in (("q_sb", WIDTH), ("k_sb", WIDTH), ("v_sb", WIDTH), ("g_sb", WIDTH), ("u_pool", WIDTH),
                       ("g_pool", WIDTH), ("u_lru", WIDTH), ("g_lru", WIDTH), ("q_xa", WIDTH), ("g_xa", WIDTH),
                       ("merge", N_BRANCH * dm)):
        cols[name] = w_in[:, off:off + size]
        off += size
    qscale = LOG2E / math.sqrt(HEAD_DIM)
    w_a = jnp.concatenate([cols[n] for n in ("q_sb", "k_sb", "v_sb", "q_xa")], axis=1).astype(BF16)
    scale_a = jnp.concatenate([jnp.full((1, WIDTH), qscale, F32), jnp.ones((1, 2 * WIDTH), F32),
                               jnp.full((1, WIDTH), qscale, F32)], axis=1)
    w_gate = jnp.concatenate([cols[n] for n in ("g_sb", "g_pool", "g_lru", "g_xa")], axis=1).astype(BF16)
    w_u = jnp.concatenate([cols[n] for n in ("u_pool", "u_lru")], axis=1).astype(BF16)
    qkv = _matmul(xf, w_a, scale_a, BF16, tm, tn)
    kv = _matmul(memf, w_mem_kv.astype(BF16), jnp.ones((1, 2 * WIDTH), F32), BF16, tm, tn)
    ysb = _sb_attention(qkv.reshape(batch, s, 4 * WIDTH), tq, group, sub).reshape(batch * s, WIDTH)
    row = lambda v: v.reshape(1, -1).astype(F32)
    wts = (cols["merge"].astype(BF16), w_gate, w_u,
           w_pool.astype(BF16), row(pool_scale), conv_w.astype(F32), row(conv_b),
           _block_diag_tiles(w_rg, MXU_TILE).astype(BF16), row(b_rg),
           _block_diag_tiles(w_ig, MXU_TILE).astype(BF16), row(b_ig),
           row(lru_L), w_branch.astype(BF16), w_out.astype(BF16), row(ln_g), row(ln_b))
    return _mixer(xf, qkv, ysb, kv, wts, batch, t, (2 * depth) ** 0.25)


def kernel(x, mem, w_in, w_pool, pool_scale, conv_w, conv_b, w_rg, b_rg, w_ig, b_ig, lru_L, w_mem_kv, w_branch,
           w_out, ln_g, ln_b):
    b, s, dm = x.shape
    depth = w_in.shape[0]
    xf = x.reshape(b * s, dm)
    memf = mem.reshape(-1, dm)
    for l in range(depth):
        xf = _layer(xf, memf, b, depth, w_in[l], w_pool[l], pool_scale[l], conv_w[l], conv_b[l], w_rg[l], b_rg[l],
                    w_ig[l], b_ig[l], lru_L[l], w_mem_kv[l], w_branch[l], w_out[l], ln_g[l], ln_b[l])
    return xf.reshape(b, s, dm)
```

```python
import functools
import math

import jax
import jax.numpy as jnp
from jax import lax
from jax.experimental import pallas as pl
from jax.experimental.pallas import tpu as pltpu

F32 = jnp.float32
BF16 = jnp.bfloat16

HEAD_DIM = 128
N_HEADS = 4
WIDTH = N_HEADS * HEAD_DIM
N_BRANCH = 4
POOL_WINDOWS = (2, 4, 8, 16)
LRU_C = 8.0
CONV_WIDTH = 4
LN_EPS = 1e-5
SUBLANES = 8
MXU_TILE = 256
QKV_XA_BLOCK = 3
HALO = 16
LOG2E = 1.4426950408889634
SOFTPLUS_CLAMP = 64.0
STICK_EXHAUSTED_BITS = 192.0
VMEM_LIMIT = 56 * 1024 * 1024


def _cparams(sem):
    return pltpu.CompilerParams(dimension_semantics=sem, vmem_limit_bytes=VMEM_LIMIT)


def _mm_kernel(x_ref, w_ref, s_ref, o_ref):
    acc = jnp.dot(x_ref[...].astype(BF16), w_ref[...], preferred_element_type=F32)
    o_ref[...] = (acc * s_ref[...]).astype(o_ref.dtype)


def _matmul(x, w, scale, out_dtype, tm, tn):
    m, k = x.shape
    n = w.shape[1]
    tm, tn = min(tm, m), min(tn, n)
    return pl.pallas_call(
        _mm_kernel,
        grid=(m // tm, n // tn),
        in_specs=[pl.BlockSpec((tm, k), lambda i, j: (i, 0)),
                  pl.BlockSpec((k, tn), lambda i, j: (0, j)),
                  pl.BlockSpec((1, tn), lambda i, j: (0, j))],
        out_specs=pl.BlockSpec((tm, tn), lambda i, j: (i, j)),
        out_shape=jax.ShapeDtypeStruct((m, n), out_dtype),
        compiler_params=_cparams(("parallel", "parallel")),
        name="proj_matmul",
    )(x, w, scale)


def _sb_kernel(q_ref, k_ref, v_ref, t_ref, o_ref, later_ref, d_ref, tot_ref, *, tq, group, sub):
    step = pl.program_id(2)
    tmat = t_ref[...]
    row = lax.broadcasted_iota(jnp.int32, (tq, tq), 0)
    col = lax.broadcasted_iota(jnp.int32, (tq, tq), 1)
    valid = col < row
    reps = tq // HEAD_DIM

    def rows(j):
        return pl.ds(pl.multiple_of(j * tq, tq), tq)

    def scores(q, j):
        return lax.dot_general(q, k_ref[rows(j), :], (((1,), (1,)), ((), ())), preferred_element_type=F32)

    def front(z, mask):
        sp = jnp.maximum(z, jnp.log2(1.0 + jnp.exp2(jnp.minimum(z, SOFTPLUS_CLAMP))))
        if mask is not None:
            sp = jnp.where(mask, sp, 0.0)
        incl = jnp.dot(sp.astype(BF16), tmat, preferred_element_type=F32)
        return z - incl, jnp.broadcast_to(incl[:, 0:1], (tq, HEAD_DIM))

    def back(d, later, v, mask):
        w = jnp.exp2(d + jnp.concatenate([later] * reps, axis=1))
        if mask is not None:
            w = jnp.where(mask, w, 0.0)
        return jnp.dot(w.astype(BF16), v, preferred_element_type=F32)

    def tile_of(i, m):
        return jnp.maximum(i - 1 - m, 0)

    def masked_v(i, m):
        v = v_ref[rows(tile_of(i, m)), :]
        return jnp.where(m < i, v, jnp.zeros_like(v))

    tiles = []
    for s in range(sub):
        i = step * sub + s
        own = slice(s * tq, (s + 1) * tq)
        q = q_ref[own, :]
        tiles.append((i, own, scores(q, i), scores(q, tile_of(i, 0))))
    fronts = [(front(z_diag, valid), front(z_below, None)) for _, _, z_diag, z_below in tiles]
    for (i, own, _, _), ((d0, tot0), (d1, tot1)) in zip(tiles, fronts):
        later0 = -tot0
        o_ref[own, :] = (back(d0, jnp.zeros((tq, HEAD_DIM), F32), v_ref[rows(i), :], valid)
                         + back(d1, later0, masked_v(i, 0), None))
        later_ref[own, :] = later0 - tot1

    def remaining_visits(i, own):
        q = q_ref[own, :]

        def stick_left():
            return (jnp.max(later_ref[own, :]) > -STICK_EXHAUSTED_BITS).astype(jnp.int32)

        def fronts_to_scratch(g):
            for t in range(group):
                d_ref[t], tot_ref[t] = front(scores(q, tile_of(i, 1 + g * group + t)), None)

        def backs_from_scratch(g):
            later = later_ref[own, :]
            total = None
            for t in range(group):
                part = back(d_ref[t], later, masked_v(i, 1 + g * group + t), None)
                total = part if total is None else total + part
                later = later - tot_ref[t]
            o_ref[own, :] += total
            later_ref[own, :] = later

        n_groups = (jnp.maximum(i - 1, 0) + group - 1) // group
        started = jnp.where(n_groups > 0, stick_left(), 0)

        @pl.when(started == 1)
        def _():
            fronts_to_scratch(0)

        def trip(state):
            g, _ = state
            backs_from_scratch(g)
            fronts_to_scratch(g + 1)
            return g + 1, stick_left()

        last, live = lax.while_loop(lambda state: (state[1] == 1) & (state[0] + 1 < n_groups), trip,
                                    (jnp.int32(0), started))

        @pl.when(live == 1)
        def _():
            backs_from_scratch(last)

    def per_tile(s, carry):
        remaining_visits(step * sub + s, rows(s))
        return carry

    @pl.when(jnp.max(later_ref[...]) > -STICK_EXHAUSTED_BITS)
    def _():
        lax.fori_loop(0, sub, per_tile, 0)


def _sb_attention(qkv, tq, group, sub):
    b, s, _ = qkv.shape
    tq = min(tq, s)
    sub = min(sub, s // tq)
    assert s % (sub * tq) == 0 and tq % HEAD_DIM == 0
    loc = jnp.arange(tq)
    tmat = (loc[:, None] >= loc[None, :]).astype(BF16)
    return pl.pallas_call(
        functools.partial(_sb_kernel, tq=tq, group=group, sub=sub),
        grid=(b, N_HEADS, s // (sub * tq)),
        in_specs=[pl.BlockSpec((None, sub * tq, HEAD_DIM), lambda bi, h, i: (bi, i, h)),
                  pl.BlockSpec((None, s, HEAD_DIM), lambda bi, h, i: (bi, 0, N_HEADS + h)),
                  pl.BlockSpec((None, s, HEAD_DIM), lambda bi, h, i: (bi, 0, 2 * N_HEADS + h)),
                  pl.BlockSpec((tq, tq), lambda bi, h, i: (0, 0))],
        out_specs=pl.BlockSpec((None, sub * tq, HEAD_DIM), lambda bi, h, i: (bi, i, h)),
        out_shape=jax.ShapeDtypeStruct((b, s, WIDTH), F32),
        scratch_shapes=[pltpu.VMEM((sub * tq, HEAD_DIM), F32),
                        pltpu.VMEM((group, tq, tq), F32), pltpu.VMEM((group, tq, HEAD_DIM), F32)],
        compiler_params=_cparams(("parallel", "parallel", "arbitrary")),
        name="stickbreak_attention",
    )(qkv, qkv, qkv, tmat)


def _sigmoid(x):
    return 0.5 + 0.5 * jnp.tanh(0.5 * x)


def _silu(x):
    half = 0.5 * x
    return half + half * jnp.tanh(half)


def _mixer_kernel(x_ref, qx_ref, ysb_ref, kv_ref, wm_ref, wg_ref, wu_ref,
                  wpool_ref, pscale_ref, convw_ref, convb_ref, wrg_ref, brg_ref, wig_ref, big_ref,
                  lrul_ref, wb_ref, wout_ref, lng_ref, lnb_ref, o_ref, h_ref, uprev_ref, *, t, alpha):
    i = pl.program_id(1)

    @pl.when(i == 0)
    def _():
        h_ref[...] = jnp.zeros_like(h_ref)

    x = x_ref[...]
    xb = x.astype(BF16)
    u = jnp.dot(xb, wu_ref[...], preferred_element_type=F32)
    gates = jnp.dot(xb, wg_ref[...], preferred_element_type=F32)
    halo = jnp.where(i == 0, 0.0, uprev_ref[...])
    uprev_ref[...] = u[t - HALO:, :]
    ext = jnp.concatenate([halo, u], axis=0)

    pos = i * t + lax.broadcasted_iota(jnp.int32, (t, HEAD_DIM), 0)
    pool_parts = []
    for g, win in enumerate(POOL_WINDOWS):
        sl = slice(g * HEAD_DIM, (g + 1) * HEAD_DIM)
        s = ext[:, sl]
        d = 1
        while d < win:
            s = s + pltpu.roll(s, d, 0)
            d *= 2
        count = jnp.minimum(pos + 1, win).astype(F32)
        pooled = s[HALO:, :] / count - u[:, sl]
        pool_parts.append(jnp.dot(pooled.astype(BF16), wpool_ref[g], preferred_element_type=F32))
    y_pool = jnp.concatenate(pool_parts, axis=1) * pscale_ref[...] * _silu(gates[:, WIDTH:2 * WIDTH])

    ul = ext[:, WIDTH:]
    cw = convw_ref[...]
    xc = cw[CONV_WIDTH - 1:CONV_WIDTH, :] * ul
    for j in range(1, CONV_WIDTH):
        xc = xc + cw[CONV_WIDTH - 1 - j:CONV_WIDTH - j, :] * pltpu.roll(ul, j, 0)
    xc = xc[HALO:, :] + convb_ref[...]
    xcb = xc.astype(BF16)
    bd = wrg_ref.shape[-1]
    r_pre = jnp.concatenate([jnp.dot(xcb[:, n * bd:(n + 1) * bd], wrg_ref[n], preferred_element_type=F32)
                             for n in range(WIDTH // bd)], axis=1)
    i_pre = jnp.concatenate([jnp.dot(xcb[:, n * bd:(n + 1) * bd], wig_ref[n], preferred_element_type=F32)
                             for n in range(WIDTH // bd)], axis=1)
    r = _sigmoid(r_pre + brg_ref[...])
    ig = _sigmoid(i_pre + big_ref[...])
    neg_l = -lrul_ref[...]
    softplus_neg_l = jnp.maximum(neg_l, 0.0) + jnp.log(1.0 + jnp.exp(-jnp.abs(neg_l)))
    log_a = (-LRU_C * r) * softplus_neg_l
    a = jnp.exp(log_a)
    th = jnp.tanh(log_a)
    bx = jnp.sqrt(-2.0 * th / (1.0 - th)) * (ig * xc)
    in_group = lax.broadcasted_iota(jnp.int32, (t, WIDTH), 0) % SUBLANES
    d = 1
    while d < SUBLANES:
        keep = in_group >= d
        bx = jnp.where(keep, a * pltpu.roll(bx, d, 0) + bx, bx)
        a = jnp.where(keep, a * pltpu.roll(a, d, 0), a)
        d *= 2
    carry = h_ref[...]
    pieces = []
    for g in range(t // SUBLANES):
        sl = slice(g * SUBLANES, (g + 1) * SUBLANES)
        piece = bx[sl, :] + a[sl, :] * carry
        pieces.append(piece)
        carry = piece[SUBLANES - 1:SUBLANES, :]
    h = jnp.concatenate(pieces, axis=0)
    h_ref[...] = carry
    y_lru = h * _silu(gates[:, 2 * WIDTH:3 * WIDTH])

    kv = kv_ref[...]
    qx = qx_ref[...]
    xa_parts = []
    for hh in range(N_HEADS):
        sl = slice(hh * HEAD_DIM, (hh + 1) * HEAD_DIM)
        sc = lax.dot_general(qx[:, sl], kv[:, sl], (((1,), (1,)), ((), ())), preferred_element_type=F32)
        p = jnp.exp2(sc - jnp.max(sc, axis=-1, keepdims=True))
        denom = jnp.sum(p, axis=-1, keepdims=True)
        vh = kv[:, WIDTH + hh * HEAD_DIM:WIDTH + (hh + 1) * HEAD_DIM]
        xa_parts.append(jnp.dot(p.astype(BF16), vh, preferred_element_type=F32) / denom)
    y_xa = jnp.concatenate(xa_parts, axis=1) * _silu(gates[:, 3 * WIDTH:])

    y_sb = ysb_ref[...] * _silu(gates[:, :WIDTH])

    dm = o_ref.shape[-1]
    merged = None
    for n, y in enumerate((y_sb, y_pool, y_lru, y_xa)):
        gate_pre = jnp.dot(xb, wm_ref[:, n * dm:(n + 1) * dm], preferred_element_type=F32)
        term = _sigmoid(gate_pre) * jnp.dot(y.astype(BF16), wb_ref[n], preferred_element_type=F32)
        merged = term if merged is None else merged + term
    out = jnp.dot(merged.astype(BF16), wout_ref[...], preferred_element_type=F32)
    resid = alpha * x + out
    mean = jnp.mean(resid, axis=-1, keepdims=True)
    cen = resid - mean
    var = jnp.mean(cen * cen, axis=-1, keepdims=True)
    o_ref[...] = cen * lax.rsqrt(var + LN_EPS) * lng_ref[...] + lnb_ref[...]


def _mixer(xf, qkv, ysb, kv, wts, batch, t, alpha):
    m, dm = xf.shape
    s = m // batch
    t = min(t, s)
    nt = s // t
    mem_len = kv.shape[0] // batch
    assert s % t == 0 and t % HALO == 0
    row = lambda b, i: b * nt + i

    def resident(arr):
        return pl.BlockSpec(arr.shape, lambda b, i: (0,) * arr.ndim, pipeline_mode=pl.Buffered(1))

    in_specs = [
        pl.BlockSpec((t, dm), lambda b, i: (row(b, i), 0)),
        pl.BlockSpec((t, WIDTH), lambda b, i: (row(b, i), QKV_XA_BLOCK)),
        pl.BlockSpec((t, WIDTH), lambda b, i: (row(b, i), 0)),
        pl.BlockSpec((mem_len, 2 * WIDTH), lambda b, i: (b, 0)),
    ] + [resident(w) for w in wts]
    return pl.pallas_call(
        functools.partial(_mixer_kernel, t=t, alpha=alpha),
        grid=(batch, nt),
        in_specs=in_specs,
        out_specs=pl.BlockSpec((t, dm), lambda b, i: (row(b, i), 0)),
        out_shape=jax.ShapeDtypeStruct((m, dm), F32),
        scratch_shapes=[pltpu.VMEM((1, WIDTH), F32), pltpu.VMEM((HALO, 2 * WIDTH), F32)],
        compiler_params=_cparams(("arbitrary", "arbitrary")),
        name="mixer",
    )(xf, qkv, ysb, kv, *wts)


def _block_diag_tiles(w, tile):
    h, n, _ = w.shape
    per = tile // n
    eye = jnp.eye(per, dtype=w.dtype)
    return jnp.einsum('gpij,pq->gpiqj', w.reshape(h // per, per, n, n), eye).reshape(h // per, tile, tile)


def _layer(xf, memf, batch, depth, w_in, w_pool, pool_scale, conv_w, conv_b, w_rg, b_rg, w_ig, b_ig, lru_L,
           w_mem_kv, w_branch, w_out, ln_g, ln_b, *, tq=256, group=8, sub=8, t=512, tm=1024, tn=2048):
    dm = xf.shape[1]
    s = xf.shape[0] // batch
    w_in = w_in.astype(BF16)
    cols = {}
    off = 0
    for name, size in (("q_sb", WIDTH), ("k_sb", WIDTH), ("v_sb", WIDTH), ("g_sb", WIDTH), ("u_pool", WIDTH),
                       ("g_pool", WIDTH), ("u_lru", WIDTH), ("g_lru", WIDTH), ("q_xa", WIDTH), ("g_xa", WIDTH),
                       ("merge", N_BRANCH * dm)):
        cols[name] = w_in[:, off:off + size]
        off += size
    qscale = LOG2E / math.sqrt(HEAD_DIM)
    w_a = jnp.concatenate([cols[n] for n in ("q_sb", "k_sb", "v_sb", "q_xa")], axis=1).astype(BF16)
    scale_a = jnp.concatenate([jnp.full((1, WIDTH), qscale, F32), jnp.ones((1, 2 * WIDTH), F32),
                               jnp.full((1, WIDTH), qscale, F32)], axis=1)
    w_gate = jnp.concatenate([cols[n] for n in ("g_sb", "g_pool", "g_lru", "g_xa")], axis=1).astype(BF16)
    w_u = jnp.concatenate([cols[n] for n in ("u_pool", "u_lru")], axis=1).astype(BF16)
    qkv = _matmul(xf, w_a, scale_a, BF16, tm, tn)
    kv = _matmul(memf, w_mem_kv.astype(BF16), jnp.ones((1, 2 * WIDTH), F32), BF16, tm, tn)
    ysb = _sb_attention(qkv.reshape(batch, s, 4 * WIDTH), tq, group, sub).reshape(batch * s, WIDTH)
    row = lambda v: v.reshape(1, -1).astype(F32)
    wts = (cols["merge"].astype(BF16), w_gate, w_u,
           w_pool.astype(BF16), row(pool_scale), conv_w.astype(F32), row(conv_b),
           _block_diag_tiles(w_rg, MXU_TILE).astype(BF16), row(b_rg),
           _block_diag_tiles(w_ig, MXU_TILE).astype(BF16), row(b_ig),
           row(lru_L), w_branch.astype(BF16), w_out.astype(BF16), row(ln_g), row(ln_b))
    return _mixer(xf, qkv, ysb, kv, wts, batch, t, (2 * depth) ** 0.25)


def kernel(x, mem, w_in, w_pool, pool_scale, conv_w, conv_b, w_rg, b_rg, w_ig, b_ig, lru_L, w_mem_kv, w_branch,
           w_out, ln_g, ln_b):
    b, s, dm = x.shape
    depth = w_in.shape[0]
    xf = x.reshape(b * s, dm)
    memf = mem.reshape(-1, dm)
    for l in range(depth):
        xf = _layer(xf, memf, b, depth, w_in[l], w_pool[l], pool_scale[l], conv_w[l], conv_b[l], w_rg[l], b_rg[l],
                    w_ig[l], b_ig[l], lru_L[l], w_mem_kv[l], w_branch[l], w_out[l], ln_g[l], ln_b[l])
    return xf.reshape(b, s, dm)
```

```python
import functools
import math

import jax
import jax.numpy as jnp
from jax import lax
from jax.experimental import pallas as pl
from jax.experimental.pallas import tpu as pltpu

F32 = jnp.float32
BF16 = jnp.bfloat16

HEAD_DIM = 128
N_HEADS = 4
WIDTH = N_HEADS * HEAD_DIM
N_BRANCH = 4
POOL_WINDOWS = (2, 4, 8, 16)
LRU_C = 8.0
CONV_WIDTH = 4
LN_EPS = 1e-5
SUBLANES = 8
MXU_TILE = 256
QKV_XA_BLOCK = 3
HALO = 16
LOG2E = 1.4426950408889634
SOFTPLUS_CLAMP = 64.0
STICK_EXHAUSTED_BITS = 192.0
VMEM_LIMIT = 56 * 1024 * 1024


def _cparams(sem):
    return pltpu.CompilerParams(dimension_semantics=sem, vmem_limit_bytes=VMEM_LIMIT)


def _mm_kernel(x_ref, w_ref, s_ref, o_ref):
    acc = jnp.dot(x_ref[...].astype(BF16), w_ref[...], preferred_element_type=F32)
    o_ref[...] = (acc * s_ref[...]).astype(o_ref.dtype)


def _matmul(x, w, scale, out_dtype, tm, tn):
    m, k = x.shape
    n = w.shape[1]
    tm, tn = min(tm, m), min(tn, n)
    return pl.pallas_call(
        _mm_kernel,
        grid=(m // tm, n // tn),
        in_specs=[pl.BlockSpec((tm, k), lambda i, j: (i, 0)),
                  pl.BlockSpec((k, tn), lambda i, j: (0, j)),
                  pl.BlockSpec((1, tn), lambda i, j: (0, j))],
        out_specs=pl.BlockSpec((tm, tn), lambda i, j: (i, j)),
        out_shape=jax.ShapeDtypeStruct((m, n), out_dtype),
        compiler_params=_cparams(("parallel", "parallel")),
        name="proj_matmul",
    )(x, w, scale)


def _sb_kernel(q_ref, k_ref, v_ref, t_ref, o_ref, later_ref, d_ref, tot_ref, *, tq, group, sub):
    step = pl.program_id(2)
    tmat = t_ref[...]
    row = lax.broadcasted_iota(jnp.int32, (tq, tq), 0)
    col = lax.broadcasted_iota(jnp.int32, (tq, tq), 1)
    valid = col < row
    reps = tq // HEAD_DIM

    def rows(j):
        return pl.ds(pl.multiple_of(j * tq, tq), tq)

    def scores(q, j):
        return lax.dot_general(q, k_ref[rows(j), :], (((1,), (1,)), ((), ())), preferred_element_type=F32)

    def front(z, mask):
        sp = jnp.maximum(z, jnp.log2(1.0 + jnp.exp2(jnp.minimum(z, SOFTPLUS_CLAMP))))
        if mask is not None:
            sp = jnp.where(mask, sp, 0.0)
        incl = jnp.dot(sp.astype(BF16), tmat, preferred_element_type=F32)
        return z - incl, jnp.broadcast_to(incl[:, 0:1], (tq, HEAD_DIM))

    def back(d, later, v, mask):
        w = jnp.exp2(d + jnp.concatenate([later] * reps, axis=1))
        if mask is not None:
            w = jnp.where(mask, w, 0.0)
        return jnp.dot(w.astype(BF16), v, preferred_element_type=F32)

    def tile_of(i, m):
        return jnp.maximum(i - 1 - m, 0)

    def masked_v(i, m):
        v = v_ref[rows(tile_of(i, m)), :]
        return jnp.where(m < i, v, jnp.zeros_like(v))

    tiles = []
    for s in range(sub):
        i = step * sub + s
        own = slice(s * tq, (s + 1) * tq)
        q = q_ref[own, :]
        tiles.append((i, own, scores(q, i), scores(q, tile_of(i, 0))))
    fronts = [(front(z_diag, valid), front(z_below, None)) for _, _, z_diag, z_below in tiles]
    for (i, own, _, _), ((d0, tot0), (d1, tot1)) in zip(tiles, fronts):
        later0 = -tot0
        o_ref[own, :] = (back(d0, jnp.zeros((tq, HEAD_DIM), F32), v_ref[rows(i), :], valid)
                         + back(d1, later0, masked_v(i, 0), None))
        later_ref[own, :] = later0 - tot1

    def remaining_visits(i, own):
        q = q_ref[own, :]

        def stick_left():
            return (jnp.max(later_ref[own, :]) > -STICK_EXHAUSTED_BITS).astype(jnp.int32)

        def fronts_to_scratch(g):
            for t in range(group):
                d_ref[t], tot_ref[t] = front(scores(q, tile_of(i, 1 + g * group + t)), None)

        def backs_from_scratch(g):
            later = later_ref[own, :]
            total = None
            for t in range(group):
                part = back(d_ref[t], later, masked_v(i, 1 + g * group + t), None)
                total = part if total is None else total + part
                later = later - tot_ref[t]
            o_ref[own, :] += total
            later_ref[own, :] = later

        n_groups = (jnp.maximum(i - 1, 0) + group - 1) // group
        started = jnp.where(n_groups > 0, stick_left(), 0)

        @pl.when(started == 1)
        def _():
            fronts_to_scratch(0)

        def trip(state):
            g, _ = state
            backs_from_scratch(g)
            fronts_to_scratch(g + 1)
            return g + 1, stick_left()

        last, live = lax.while_loop(lambda state: (state[1] == 1) & (state[0] + 1 < n_groups), trip,
                                    (jnp.int32(0), started))

        @pl.when(live == 1)
        def _():
            backs_from_scratch(last)

    def per_tile(s, carry):
        remaining_visits(step * sub + s, rows(s))
        return carry

    @pl.when(jnp.max(later_ref[...]) > -STICK_EXHAUSTED_BITS)
    def _():
        lax.fori_loop(0, sub, per_tile, 0)


def _sb_attention(qkv, tq, group, sub):
    b, s, _ = qkv.shape
    tq = min(tq, s)
    sub = min(sub, s // tq)
    assert s % (sub * tq) == 0 and tq % HEAD_DIM == 0
    loc = jnp.arange(tq)
    tmat = (loc[:, None] >= loc[None, :]).astype(BF16)
    return pl.pallas_call(
        functools.partial(_sb_kernel, tq=tq, group=group, sub=sub),
        grid=(b, N_HEADS, s // (sub * tq)),
        in_specs=[pl.BlockSpec((None, sub * tq, HEAD_DIM), lambda bi, h, i: (bi, i, h)),
                  pl.BlockSpec((None, s, HEAD_DIM), lambda bi, h, i: (bi, 0, N_HEADS + h)),
                  pl.BlockSpec((None, s, HEAD_DIM), lambda bi, h, i: (bi, 0, 2 * N_HEADS + h)),
                  pl.BlockSpec((tq, tq), lambda bi, h, i: (0, 0))],
        out_specs=pl.BlockSpec((None, sub * tq, HEAD_DIM), lambda bi, h, i: (bi, i, h)),
        out_shape=jax.ShapeDtypeStruct((b, s, WIDTH), F32),
        scratch_shapes=[pltpu.VMEM((sub * tq, HEAD_DIM), F32),
                        pltpu.VMEM((group, tq, tq), F32), pltpu.VMEM((group, tq, HEAD_DIM), F32)],
        compiler_params=_cparams(("parallel", "parallel", "arbitrary")),
        name="stickbreak_attention",
    )(qkv, qkv, qkv, tmat)


def _sigmoid(x):
    return 0.5 + 0.5 * jnp.tanh(0.5 * x)


def _silu(x):
    half = 0.5 * x
    return half + half * jnp.tanh(half)


def _mixer_kernel(x_ref, qx_ref, ysb_ref, kv_ref, wm_ref, wg_ref, wu_ref,
                  wpool_ref, pscale_ref, convw_ref, convb_ref, wrg_ref, brg_ref, wig_ref, big_ref,
                  lrul_ref, wb_ref, wout_ref, lng_ref, lnb_ref, o_ref, h_ref, uprev_ref, *, t, alpha):
    i = pl.program_id(1)

    @pl.when(i == 0)
    def _():
        h_ref[...] = jnp.zeros_like(h_ref)

    x = x_ref[...]
    xb = x.astype(BF16)
    u = jnp.dot(xb, wu_ref[...], preferred_element_type=F32)
    gates = jnp.dot(xb, wg_ref[...], preferred_element_type=F32)
    halo = jnp.where(i == 0, 0.0, uprev_ref[...])
    uprev_ref[...] = u[t - HALO:, :]
    ext = jnp.concatenate([halo, u], axis=0)

    pos = i * t + lax.broadcasted_iota(jnp.int32, (t, HEAD_DIM), 0)
    pool_parts = []
    for g, win in enumerate(POOL_WINDOWS):
        sl = slice(g * HEAD_DIM, (g + 1) * HEAD_DIM)
        s = ext[:, sl]
        d = 1
        while d < win:
            s = s + pltpu.roll(s, d, 0)
            d *= 2
        count = jnp.minimum(pos + 1, win).astype(F32)
        pooled = s[HALO:, :] / count - u[:, sl]
        pool_parts.append(jnp.dot(pooled.astype(BF16), wpool_ref[g], preferred_element_type=F32))
    y_pool = jnp.concatenate(pool_parts, axis=1) * pscale_ref[...] * _silu(gates[:, WIDTH:2 * WIDTH])

    dm = o_ref.shape[-1]
    quarter = t // 4

    def merge_gate_rows(part):
        return jnp.dot(xb[part * quarter:(part + 1) * quarter, :], wm_ref[:, :dm], preferred_element_type=F32)

    early_gate = []

    ul = ext[:, WIDTH:]
    cw = convw_ref[...]
    xc = cw[CONV_WIDTH - 1:CONV_WIDTH, :] * ul
    for j in range(1, CONV_WIDTH):
        xc = xc + cw[CONV_WIDTH - 1 - j:CONV_WIDTH - j, :] * pltpu.roll(ul, j, 0)
    xc = xc[HALO:, :] + convb_ref[...]
    early_gate.append(merge_gate_rows(0))
    xcb = xc.astype(BF16)
    bd = wrg_ref.shape[-1]
    r_pre = jnp.concatenate([jnp.dot(xcb[:, n * bd:(n + 1) * bd], wrg_ref[n], preferred_element_type=F32)
                             for n in range(WIDTH // bd)], axis=1)
    i_pre = jnp.concatenate([jnp.dot(xcb[:, n * bd:(n + 1) * bd], wig_ref[n], preferred_element_type=F32)
                             for n in range(WIDTH // bd)], axis=1)
    r = _sigmoid(r_pre + brg_ref[...])
    ig = _sigmoid(i_pre + big_ref[...])
    neg_l = -lrul_ref[...]
    softplus_neg_l = jnp.maximum(neg_l, 0.0) + jnp.log(1.0 + jnp.exp(-jnp.abs(neg_l)))
    log_a = (-LRU_C * r) * softplus_neg_l
    a = jnp.exp(log_a)
    th = jnp.tanh(log_a)
    bx = jnp.sqrt(-2.0 * th / (1.0 - th)) * (ig * xc)
    early_gate.append(merge_gate_rows(1))
    in_group = lax.broadcasted_iota(jnp.int32, (t, WIDTH), 0) % SUBLANES
    d = 1
    while d < SUBLANES:
        keep = in_group >= d
        bx = jnp.where(keep, a * pltpu.roll(bx, d, 0) + bx, bx)
        a = jnp.where(keep, a * pltpu.roll(a, d, 0), a)
        d *= 2
    early_gate.append(merge_gate_rows(2))
    carry = h_ref[...]
    pieces = []
    for g in range(t // SUBLANES):
        sl = slice(g * SUBLANES, (g + 1) * SUBLANES)
        piece = bx[sl, :] + a[sl, :] * carry
        pieces.append(piece)
        carry = piece[SUBLANES - 1:SUBLANES, :]
    h = jnp.concatenate(pieces, axis=0)
    h_ref[...] = carry
    early_gate.append(merge_gate_rows(3))
    y_lru = h * _silu(gates[:, 2 * WIDTH:3 * WIDTH])

    kv = kv_ref[...]
    qx = qx_ref[...]
    xa_parts = []
    for hh in range(N_HEADS):
        sl = slice(hh * HEAD_DIM, (hh + 1) * HEAD_DIM)
        sc = lax.dot_general(qx[:, sl], kv[:, sl], (((1,), (1,)), ((), ())), preferred_element_type=F32)
        p = jnp.exp2(sc - jnp.max(sc, axis=-1, keepdims=True))
        denom = jnp.sum(p, axis=-1, keepdims=True)
        vh = kv[:, WIDTH + hh * HEAD_DIM:WIDTH + (hh + 1) * HEAD_DIM]
        xa_parts.append(jnp.dot(p.astype(BF16), vh, preferred_element_type=F32) / denom)
    y_xa = jnp.concatenate(xa_parts, axis=1) * _silu(gates[:, 3 * WIDTH:])

    y_sb = ysb_ref[...] * _silu(gates[:, :WIDTH])

    merged = None
    for n, y in enumerate((y_sb, y_pool, y_lru, y_xa)):
        gate_pre = (jnp.concatenate(early_gate, axis=0) if n == 0 else
                    jnp.dot(xb, wm_ref[:, n * dm:(n + 1) * dm], preferred_element_type=F32))
        term = _sigmoid(gate_pre) * jnp.dot(y.astype(BF16), wb_ref[n], preferred_element_type=F32)
        merged = term if merged is None else merged + term
    out = jnp.dot(merged.astype(BF16), wout_ref[...], preferred_element_type=F32)
    resid = alpha * x + out
    mean = jnp.mean(resid, axis=-1, keepdims=True)
    cen = resid - mean
    var = jnp.mean(cen * cen, axis=-1, keepdims=True)
    o_ref[...] = cen * lax.rsqrt(var + LN_EPS) * lng_ref[...] + lnb_ref[...]


def _mixer(xf, qkv, ysb, kv, wts, batch, t, alpha):
    m, dm = xf.shape
    s = m // batch
    t = min(t, s)
    nt = s // t
    mem_len = kv.shape[0] // batch
    assert s % t == 0 and t % HALO == 0
    row = lambda b, i: b * nt + i

    def resident(arr):
        return pl.BlockSpec(arr.shape, lambda b, i: (0,) * arr.ndim, pipeline_mode=pl.Buffered(1))

    in_specs = [
        pl.BlockSpec((t, dm), lambda b, i: (row(b, i), 0)),
        pl.BlockSpec((t, WIDTH), lambda b, i: (row(b, i), QKV_XA_BLOCK)),
        pl.BlockSpec((t, WIDTH), lambda b, i: (row(b, i), 0)),
        pl.BlockSpec((mem_len, 2 * WIDTH), lambda b, i: (b, 0)),
    ] + [resident(w) for w in wts]
    return pl.pallas_call(
        functools.partial(_mixer_kernel, t=t, alpha=alpha),
        grid=(batch, nt),
        in_specs=in_specs,
        out_specs=pl.BlockSpec((t, dm), lambda b, i: (row(b, i), 0)),
        out_shape=jax.ShapeDtypeStruct((m, dm), F32),
        scratch_shapes=[pltpu.VMEM((1, WIDTH), F32), pltpu.VMEM((HALO, 2 * WIDTH), F32)],
        compiler_params=_cparams(("arbitrary", "arbitrary")),
        name="mixer",
    )(xf, qkv, ysb, kv, *wts)


def _block_diag_tiles(w, tile):
    h, n, _ = w.shape
    per = tile // n
    eye = jnp.eye(per, dtype=w.dtype)
    return jnp.einsum('gpij,pq->gpiqj', w.reshape(h // per, per, n, n), eye).reshape(h // per, tile, tile)


def _layer(xf, memf, batch, depth, w_in, w_pool, pool_scale, conv_w, conv_b, w_rg, b_rg, w_ig, b_ig, lru_L,
           w_mem_kv, w_branch, w_out, ln_g, ln_b, *, tq=256, group=8, sub=8, t=512, tm=1024, tn=2048):
    dm = xf.shape[1]
    s = xf.shape[0] // batch
    w_in = w_in.astype(BF16)
    cols = {}
    off = 0
    for name, size in (("q_sb", WIDTH), ("k_sb", WIDTH), ("v_sb", WIDTH), ("g_sb", WIDTH), ("u_pool", WIDTH),
                       ("g_pool", WIDTH), ("u_lru", WIDTH), ("g_lru", WIDTH), ("q_xa", WIDTH), ("g_xa", WIDTH),
                       ("merge", N_BRANCH * dm)):
        cols[name] = w_in[:, off:off + size]
        off += size
    qscale = LOG2E / math.sqrt(HEAD_DIM)
    w_a = jnp.concatenate([cols[n] for n in ("q_sb", "k_sb", "v_sb", "q_xa")], axis=1).astype(BF16)
    scale_a = jnp.concatenate([jnp.full((1, WIDTH), qscale, F32), jnp.ones((1, 2 * WIDTH), F32),
                               jnp.full((1, WIDTH), qscale, F32)], axis=1)
    w_gate = jnp.concatenate([cols[n] for n in ("g_sb", "g_pool", "g_lru", "g_xa")], axis=1).astype(BF16)
    w_u = jnp.concatenate([cols[n] for n in ("u_pool", "u_lru")], axis=1).astype(BF16)
    qkv = _matmul(xf, w_a, scale_a, BF16, tm, tn)
    kv = _matmul(memf, w_mem_kv.astype(BF16), jnp.ones((1, 2 * WIDTH), F32), BF16, tm, tn)
    ysb = _sb_attention(qkv.reshape(batch, s, 4 * WIDTH), tq, group, sub).reshape(batch * s, WIDTH)
    row = lambda v: v.reshape(1, -1).astype(F32)
    wts = (cols["merge"].astype(BF16), w_gate, w_u,
           w_pool.astype(BF16), row(pool_scale), conv_w.astype(F32), row(conv_b),
           _block_diag_tiles(w_rg, MXU_TILE).astype(BF16), row(b_rg),
           _block_diag_tiles(w_ig, MXU_TILE).astype(BF16), row(b_ig),
           row(lru_L), w_branch.astype(BF16), w_out.astype(BF16), row(ln_g), row(ln_b))
    return _mixer(xf, qkv, ysb, kv, wts, batch, t, (2 * depth) ** 0.25)


def kernel(x, mem, w_in, w_pool, pool_scale, conv_w, conv_b, w_rg, b_rg, w_ig, b_ig, lru_L, w_mem_kv, w_branch,
           w_out, ln_g, ln_b):
    b, s, dm = x.shape
    depth = w_in.shape[0]
    xf = x.reshape(b * s, dm)
    memf = mem.reshape(-1, dm)
    for l in range(depth):
        xf = _layer(xf, memf, b, depth, w_in[l], w_pool[l], pool_scale[l], conv_w[l], conv_b[l], w_rg[l], b_rg[l],
                    w_ig[l], b_ig[l], lru_L[l], w_mem_kv[l], w_branch[l], w_out[l], ln_g[l], ln_b[l])
    return xf.reshape(b, s, dm)
```
